```python
import math
import jax, jax.numpy as jnp
from jax import lax
import numpy as np

D_MODEL = 1024
BATCH = 8
SEQ = 4096
DEPTH = 4

N_MIXERS = 3
EPS = 1e-6
D_FF = 2816

GLA_HEADS = 4
GLA_DK = D_MODEL // 2
GLA_DV = D_MODEL
GLA_DK_H = GLA_DK // GLA_HEADS
GLA_DV_H = GLA_DV // GLA_HEADS
GLA_GATE_RANK = 16
GLA_GATE_TEMP = 16.0
GLA_CHUNK = 64
GLA_IN = 2 * GLA_DK + 2 * GLA_DV + GLA_GATE_RANK

SSD_D_INNER = 2 * D_MODEL
SSD_HEAD_DIM = 64
SSD_HEADS = SSD_D_INNER // SSD_HEAD_DIM
SSD_GROUPS = 4
SSD_HPG = SSD_HEADS // SSD_GROUPS
SSD_STATE = 128
SSD_CONV = 4
SSD_CHUNK = 64
SSD_CONV_DIM = SSD_D_INNER + 2 * SSD_GROUPS * SSD_STATE
SSD_IN = SSD_D_INNER + SSD_CONV_DIM + SSD_HEADS

MLA_HEADS = 8
MLA_Q_LORA = 384
MLA_KV_LORA = 256
MLA_NOPE = 128
MLA_ROPE = 64
MLA_V = 128
MLA_QK = MLA_NOPE + MLA_ROPE
MLA_IN = MLA_Q_LORA + MLA_KV_LORA + MLA_ROPE
MLA_BLOCK = 128
ROPE_THETA = 10000.0

kernel_name = "hybrid_gla_ssd_mla_macaron"


def rmsnorm(x, w):
    xf = x.astype(jnp.float32)
    y = xf * lax.rsqrt(jnp.mean(xf * xf, axis=-1, keepdims=True) + EPS)
    return (y * w.astype(jnp.float32)).astype(x.dtype)


def swiglu(x, w_gu, w_down):
    g, u = jnp.split(x @ w_gu, 2, axis=-1)
    return (jax.nn.silu(g) * u) @ w_down


def gla_mixer(h, w_in, w_gate_b, b_gate, gn_w, w_out):
    bsz, s, _ = h.shape
    nc = s // GLA_CHUNK
    proj = h @ w_in
    q, k, v, g, a_lr = jnp.split(
        proj, [GLA_DK, 2 * GLA_DK, 2 * GLA_DK + GLA_DV, 2 * GLA_DK + 2 * GLA_DV], axis=-1)
    log_a = jax.nn.log_sigmoid((a_lr @ w_gate_b + b_gate).astype(jnp.float32)) / GLA_GATE_TEMP

    def heads(t, d):
        return t.reshape(bsz, nc, GLA_CHUNK, GLA_HEADS, d).astype(jnp.float32)

    q = heads(q, GLA_DK_H) * (GLA_DK_H ** -0.5)
    k = heads(k, GLA_DK_H)
    v = heads(v, GLA_DV_H)
    b = jnp.cumsum(heads(log_a, GLA_DK_H), axis=2)
    b_last = b[:, :, -1:]
    q_dec = q * jnp.exp(b)
    k_inv = k * jnp.exp(-b)
    k_end = k * jnp.exp(b_last - b)

    causal = jnp.tril(jnp.ones((GLA_CHUNK, GLA_CHUNK), dtype=bool))
    attn = jnp.einsum("bnihd,bnjhd->bnhij", q_dec, k_inv)
    attn = jnp.where(causal, attn, 0.0)
    o_intra = jnp.einsum("bnhij,bnjhv->bnihv", attn, v)

    def step(state, inp):
        qd, ke, vv, dl = inp
        o = jnp.einsum("bihd,bhdv->bihv", qd, state)
        state = state * dl[..., None] + jnp.einsum("bjhd,bjhv->bhdv", ke, vv)
        return state, o

    xs = (jnp.moveaxis(q_dec, 1, 0), jnp.moveaxis(k_end, 1, 0), jnp.moveaxis(v, 1, 0),
          jnp.moveaxis(jnp.exp(b_last[:, :, 0]), 1, 0))
    state0 = jnp.zeros((bsz, GLA_HEADS, GLA_DK_H, GLA_DV_H), jnp.float32)
    _, o_inter = lax.scan(step, state0, xs)
    o = (o_intra + jnp.moveaxis(o_inter, 0, 1)).reshape(bsz, s, GLA_HEADS, GLA_DV_H)

    o = o * lax.rsqrt(jnp.mean(o * o, axis=-1, keepdims=True) + EPS)
    o = o.reshape(bsz, s, GLA_DV) * gn_w.astype(jnp.float32)
    o = o * jax.nn.silu(g.astype(jnp.float32))
    return o.astype(h.dtype) @ w_out


def causal_depthwise_conv(u, w, b):
    k_w = w.shape[0]
    out = lax.conv_general_dilated(
        u, w[:, None, :].astype(u.dtype), window_strides=(1,), padding=((k_w - 1, 0),),
        dimension_numbers=("NWC", "WIO", "NWC"), feature_group_count=u.shape[-1])
    return out + b


def ssd_mixer(h, w_in, conv_w, conv_b, dt_bias, a_log, d_skip, norm_w, w_out):
    bsz, s, _ = h.shape
    nc = s // SSD_CHUNK
    G, R, P, N, Q = SSD_GROUPS, SSD_HPG, SSD_HEAD_DIM, SSD_STATE, SSD_CHUNK
    proj = h @ w_in
    z, xbc, dt = jnp.split(proj, [SSD_D_INNER, SSD_D_INNER + SSD_CONV_DIM], axis=-1)
    xbc = jax.nn.silu(causal_depthwise_conv(xbc, conv_w, conv_b))
    xs, bm, cm = jnp.split(xbc, [SSD_D_INNER, SSD_D_INNER + G * N], axis=-1)

    x = xs.reshape(bsz, nc, Q, G, R, P).astype(jnp.float32)
    bm = bm.reshape(bsz, nc, Q, G, N).astype(jnp.float32)
    cm = cm.reshape(bsz, nc, Q, G, N).astype(jnp.float32)
    dt = jax.nn.softplus(dt.astype(jnp.float32) + dt_bias.astype(jnp.float32))
    dt = dt.reshape(bsz, nc, Q, G, R)
    a_neg = -jnp.exp(a_log.astype(jnp.float32)).reshape(G, R)
    a_cs = jnp.cumsum(dt * a_neg, axis=2)
    xdt = x * dt[..., None]

    causal = jnp.tril(jnp.ones((Q, Q), dtype=bool))[:, :, None, None]
    seg = a_cs[:, :, :, None] - a_cs[:, :, None, :]
    L = jnp.exp(jnp.where(causal, seg, -jnp.inf))
    cb = jnp.einsum("bcign,bcjgn->bcgij", cm, bm)
    y_diag = jnp.einsum("bcgij,bcijgr,bcjgrp->bcigrp", cb, L, xdt)

    decay_end = jnp.exp(a_cs[:, :, -1:] - a_cs)
    total = jnp.exp(a_cs[:, :, -1])

    def step(state, inp):
        c_c, b_c, xdt_c, acs_c, dec_c, tot_c = inp
        y_off = jnp.einsum("bign,bgrpn,bigr->bigrp", c_c, state, jnp.exp(acs_c))
        state = state * tot_c[..., None, None] + jnp.einsum(
            "bjgn,bjgr,bjgrp->bgrpn", b_c, dec_c, xdt_c)
        return state, y_off

    scan_in = tuple(jnp.moveaxis(t, 1, 0) for t in (cm, bm, xdt, a_cs, decay_end, total))
    state0 = jnp.zeros((bsz, G, R, P, N), jnp.float32)
    _, y_off = lax.scan(step, state0, scan_in)
    y = y_diag + jnp.moveaxis(y_off, 0, 1) + x * d_skip.astype(jnp.float32).reshape(G, R)[..., None]
    y = y.reshape(bsz, s, SSD_D_INNER)

    y = y * jax.nn.silu(z.astype(jnp.float32))
    y = y.reshape(bsz, s, G, SSD_D_INNER // G)
    y = y * lax.rsqrt(jnp.mean(y * y, axis=-1, keepdims=True) + EPS)
    y = y.reshape(bsz, s, SSD_D_INNER) * norm_w.astype(jnp.float32)
    return y.astype(h.dtype) @ w_out


def apply_rope(t, cos, sin):
    half = t.shape[-1] // 2
    t1, t2 = t[..., :half], t[..., half:]
    return jnp.concatenate([t1 * cos - t2 * sin, t1 * sin + t2 * cos], axis=-1)


def mla_mixer(h, positions, w_in, q_norm, w_uq, kv_norm, w_ukv, w_out):
    bsz, s, _ = h.shape
    proj = h @ w_in
    cq, ckv, k_rope = jnp.split(proj, [MLA_Q_LORA, MLA_Q_LORA + MLA_KV_LORA], axis=-1)
    q = (rmsnorm(cq, q_norm) @ w_uq).reshape(bsz, s, MLA_HEADS, MLA_QK).astype(jnp.float32)
    kv = (rmsnorm(ckv, kv_norm) @ w_ukv).reshape(bsz, s, MLA_HEADS, MLA_NOPE + MLA_V)
    kv = kv.astype(jnp.float32)
    q_nope, q_rope = q[..., :MLA_NOPE], q[..., MLA_NOPE:]
    k_nope, v = kv[..., :MLA_NOPE], kv[..., MLA_NOPE:]

    inv_freq = 1.0 / (ROPE_THETA ** (jnp.arange(0, MLA_ROPE, 2, dtype=jnp.float32) / MLA_ROPE))
    ang = positions.astype(jnp.float32)[..., None] * inv_freq
    cos = jnp.cos(ang)[:, :, None, :]
    sin = jnp.sin(ang)[:, :, None, :]
    q_rope = apply_rope(q_rope, cos, sin)
    k_rope = apply_rope(k_rope.astype(jnp.float32)[:, :, None, :], cos, sin)

    q = jnp.concatenate([q_nope, q_rope], axis=-1) * (MLA_QK ** -0.5)
    k = jnp.concatenate(
        [k_nope, jnp.broadcast_to(k_rope, (bsz, s, MLA_HEADS, MLA_ROPE))], axis=-1)

    nb = s // MLA_BLOCK
    qb = jnp.moveaxis(q.reshape(bsz, nb, MLA_BLOCK, MLA_HEADS, MLA_QK), 1, 0)
    key_idx = jnp.arange(s)

    def block(args):
        qi, blk = args
        sc = jnp.einsum("bqhd,bkhd->bhqk", qi, k)
        q_idx = blk * MLA_BLOCK + jnp.arange(MLA_BLOCK)
        mask = key_idx[None, :] <= q_idx[:, None]
        p = jax.nn.softmax(jnp.where(mask, sc, -jnp.inf), axis=-1)
        return jnp.einsum("bhqk,bkhv->bqhv", p, v)

    o = lax.map(block, (qb, jnp.arange(nb)))
    o = jnp.moveaxis(o, 0, 1).reshape(bsz, s, MLA_HEADS * MLA_V)
    return o.astype(h.dtype) @ w_out


def setup_inputs(seed: int = 0) -> dict:
    key = jax.random.key(seed)
    ks = iter(jax.random.split(key, 64))
    f32 = jnp.float32

    def nrm(shape, scale):
        return scale * jax.random.normal(next(ks), shape, f32)

    def gain(n):
        return 1.0 + 0.01 * jax.random.normal(next(ks), (n,), f32)

    p = {}
    p["x"] = jax.random.normal(next(ks), (BATCH, SEQ, D_MODEL), f32)
    offs = jax.random.randint(next(ks), (BATCH, 1), 0, 1024, dtype=jnp.int32)
    p["positions"] = jnp.arange(SEQ, dtype=jnp.int32)[None, :] + offs
    for l in range(DEPTH):
        pre = "l%d_" % l
        p[pre + "ffn1_norm"] = gain(D_MODEL)
        p[pre + "ffn1_w_gu"] = nrm((D_MODEL, 2 * D_FF), D_MODEL ** -0.5)
        p[pre + "ffn1_w_down"] = nrm((D_FF, D_MODEL), D_FF ** -0.5)
        p[pre + "mix_norm"] = gain(D_MODEL)
        kind = l % N_MIXERS
        if kind == 0:
            p[pre + "gla_w_in"] = nrm((D_MODEL, GLA_IN), D_MODEL ** -0.5)
            p[pre + "gla_w_gate_b"] = nrm((GLA_GATE_RANK, GLA_DK), GLA_GATE_RANK ** -0.5)
            p[pre + "gla_b_gate"] = nrm((GLA_DK,), 0.01)
            p[pre + "gla_norm"] = gain(GLA_DV)
            p[pre + "gla_w_out"] = nrm((GLA_DV, D_MODEL), GLA_DV ** -0.5)
        elif kind == 1:
            p[pre + "ssd_w_in"] = nrm((D_MODEL, SSD_IN), D_MODEL ** -0.5)
            p[pre + "ssd_conv_w"] = nrm((SSD_CONV, SSD_CONV_DIM), SSD_CONV ** -0.5)
            p[pre + "ssd_conv_b"] = nrm((SSD_CONV_DIM,), 0.01)
            dt0 = jnp.exp(jax.random.uniform(next(ks), (SSD_HEADS,), f32,
                                             math.log(1e-3), math.log(1e-1)))
            p[pre + "ssd_dt_bias"] = dt0 + jnp.log(-jnp.expm1(-dt0))
            p[pre + "ssd_a_log"] = jnp.log(jax.random.uniform(next(ks), (SSD_HEADS,), f32, 1.0, 16.0))
            p[pre + "ssd_d_skip"] = gain(SSD_HEADS)
            p[pre + "ssd_norm"] = gain(SSD_D_INNER)
            p[pre + "ssd_w_out"] = nrm((SSD_D_INNER, D_MODEL), SSD_D_INNER ** -0.5)
        else:
            p[pre + "mla_w_in"] = nrm((D_MODEL, MLA_IN), D_MODEL ** -0.5)
            p[pre + "mla_q_norm"] = gain(MLA_Q_LORA)
            p[pre + "mla_w_uq"] = nrm((MLA_Q_LORA, MLA_HEADS * MLA_QK), MLA_Q_LORA ** -0.5)
            p[pre + "mla_kv_norm"] = gain(MLA_KV_LORA)
            p[pre + "mla_w_ukv"] = nrm((MLA_KV_LORA, MLA_HEADS * (MLA_NOPE + MLA_V)), MLA_KV_LORA ** -0.5)
            p[pre + "mla_w_out"] = nrm((MLA_HEADS * MLA_V, D_MODEL), (MLA_HEADS * MLA_V) ** -0.5)
        p[pre + "ffn2_norm"] = gain(D_MODEL)
        p[pre + "ffn2_w_gu"] = nrm((D_MODEL, 2 * D_FF), D_MODEL ** -0.5)
        p[pre + "ffn2_w_down"] = nrm((D_FF, D_MODEL), D_FF ** -0.5)
    p["final_norm"] = gain(D_MODEL)
    return p


def reference(x, positions,
              l0_ffn1_norm, l0_ffn1_w_gu, l0_ffn1_w_down, l0_mix_norm,
              l0_gla_w_in, l0_gla_w_gate_b, l0_gla_b_gate, l0_gla_norm, l0_gla_w_out,
              l0_ffn2_norm, l0_ffn2_w_gu, l0_ffn2_w_down,
              l1_ffn1_norm, l1_ffn1_w_gu, l1_ffn1_w_down, l1_mix_norm,
              l1_ssd_w_in, l1_ssd_conv_w, l1_ssd_conv_b, l1_ssd_dt_bias, l1_ssd_a_log,
              l1_ssd_d_skip, l1_ssd_norm, l1_ssd_w_out,
              l1_ffn2_norm, l1_ffn2_w_gu, l1_ffn2_w_down,
              l2_ffn1_norm, l2_ffn1_w_gu, l2_ffn1_w_down, l2_mix_norm,
              l2_mla_w_in, l2_mla_q_norm, l2_mla_w_uq, l2_mla_kv_norm, l2_mla_w_ukv, l2_mla_w_out,
              l2_ffn2_norm, l2_ffn2_w_gu, l2_ffn2_w_down,
              l3_ffn1_norm, l3_ffn1_w_gu, l3_ffn1_w_down, l3_mix_norm,
              l3_gla_w_in, l3_gla_w_gate_b, l3_gla_b_gate, l3_gla_norm, l3_gla_w_out,
              l3_ffn2_norm, l3_ffn2_w_gu, l3_ffn2_w_down,
              final_norm):
    layers = [
        ((l0_ffn1_norm, l0_ffn1_w_gu, l0_ffn1_w_down), l0_mix_norm, gla_mixer,
         (l0_gla_w_in, l0_gla_w_gate_b, l0_gla_b_gate, l0_gla_norm, l0_gla_w_out),
         (l0_ffn2_norm, l0_ffn2_w_gu, l0_ffn2_w_down)),
        ((l1_ffn1_norm, l1_ffn1_w_gu, l1_ffn1_w_down), l1_mix_norm, ssd_mixer,
         (l1_ssd_w_in, l1_ssd_conv_w, l1_ssd_conv_b, l1_ssd_dt_bias, l1_ssd_a_log,
          l1_ssd_d_skip, l1_ssd_norm, l1_ssd_w_out),
         (l1_ffn2_norm, l1_ffn2_w_gu, l1_ffn2_w_down)),
        ((l2_ffn1_norm, l2_ffn1_w_gu, l2_ffn1_w_down), l2_mix_norm, mla_mixer,
         (positions, l2_mla_w_in, l2_mla_q_norm, l2_mla_w_uq, l2_mla_kv_norm, l2_mla_w_ukv,
          l2_mla_w_out),
         (l2_ffn2_norm, l2_ffn2_w_gu, l2_ffn2_w_down)),
        ((l3_ffn1_norm, l3_ffn1_w_gu, l3_ffn1_w_down), l3_mix_norm, gla_mixer,
         (l3_gla_w_in, l3_gla_w_gate_b, l3_gla_b_gate, l3_gla_norm, l3_gla_w_out),
         (l3_ffn2_norm, l3_ffn2_w_gu, l3_ffn2_w_down)),
    ]
    h = x
    for i in range(DEPTH):
        ffn1, mix_norm, mixer, mparams, ffn2 = layers[i]
        h = h + 0.5 * swiglu(rmsnorm(h, ffn1[0]), ffn1[1], ffn1[2])
        h = h + mixer(rmsnorm(h, mix_norm), *mparams)
        h = h + 0.5 * swiglu(rmsnorm(h, ffn2[0]), ffn2[1], ffn2[2])
    return rmsnorm(h, final_norm)
```

```python
import functools
import math

import jax
import jax.numpy as jnp
from jax import lax
from jax.experimental import pallas as pl
from jax.experimental.pallas import tpu as pltpu

F32 = jnp.float32
BF16 = jnp.bfloat16
EPS = 1e-6

V7X_VMEM_BYTES = 64 * 1024 * 1024
VMEM_LIMIT = V7X_VMEM_BYTES - 8 * 1024 * 1024
LANES = 128

D_FF = 2816
FF_CHUNK = 256

GLA_HEADS = 4
GLA_DK_H = 128
GLA_DV_H = 256
GLA_GATE_RANK = 16
GLA_GATE_TEMP = 16.0
GLA_CHUNK = 64

SSD_D_INNER = 2048
SSD_HEAD_DIM = 64
SSD_HEADS = 32
SSD_GROUPS = 4
SSD_HPG = 8
SSD_STATE = 128
SSD_CONV = 4
SSD_CHUNK = 64
SSD_CONV_DIM = 3072
SSD_HALO = 8

MLA_HEADS = 8
MLA_Q_LORA = 384
MLA_KV_LORA = 256
MLA_NOPE = 128
MLA_ROPE = 64
MLA_V = 128
MLA_QK = MLA_NOPE + MLA_ROPE
MLA_QK_PAD = 256
ROPE_THETA = 10000.0

NT_DIMS = (((1,), (1,)), ((), ()))
TN_DIMS = (((0,), (0,)), ((), ()))


def _params(*sem):
    return pltpu.CompilerParams(dimension_semantics=sem, vmem_limit_bytes=VMEM_LIMIT)


def _const_spec(shape):
    return pl.BlockSpec(shape, lambda *_: (0,) * len(shape), pipeline_mode=pl.Buffered(1))


def _rms(x, w):
    return x * lax.rsqrt(jnp.mean(x * x, axis=-1, keepdims=True) + EPS) * w


def _silu(x):
    return x / (1.0 + jnp.exp(-x))


def _softplus(x):
    return jnp.maximum(x, 0.0) + jnp.log(1.0 + jnp.exp(-jnp.abs(x)))


def _dot(a, b):
    return jnp.dot(a, b, preferred_element_type=F32)


def _split_bf16(a, terms):
    parts = []
    for _ in range(terms):
        p = a.astype(BF16)
        parts.append(p)
        a = a - p.astype(F32)
    return parts


def _dot_exact_rhs(a, b, terms, dims=None):
    out = None
    for p in _split_bf16(a, terms):
        if dims is None:
            t = _dot(p, b)
        else:
            t = lax.dot_general(p, b, dims, preferred_element_type=F32)
        out = t if out is None else out + t
    return out


def _transpose_via_mxu(eye, a, terms=3):
    out = None
    for p in _split_bf16(a, terms):
        t = lax.dot_general(eye, p, NT_DIMS, preferred_element_type=F32)
        out = t if out is None else out + t
    return out


def _ffn_kernel(*refs, n_chunks, tf, final):
    if final:
        x_ref, nw_ref, wgu_ref, wd_ref, fw_ref, o_ref, xn_ref, acc_ref = refs
    else:
        x_ref, nw_ref, wgu_ref, wd_ref, o_ref, xn_ref, acc_ref = refs
    xn_ref[...] = _rms(x_ref[...], nw_ref[...]).astype(BF16)
    acc_ref[...] = jnp.zeros_like(acc_ref)

    def body(c, carry):
        gu = _dot(xn_ref[...], wgu_ref[c])
        a = (_silu(gu[:, :tf]) * gu[:, tf:]).astype(BF16)
        acc_ref[...] += _dot(a, wd_ref[c])
        return carry

    lax.fori_loop(0, n_chunks, body, 0)
    y = x_ref[...] + 0.5 * acc_ref[...]
    if final:
        y = _rms(y, fw_ref[...])
    o_ref[...] = y


def _ffn(h, norm_w, w_gu, w_down, final_w=None, tm=1024):
    t, d = h.shape
    n_chunks = D_FF // FF_CHUNK
    wg = w_gu[:, :D_FF].reshape(d, n_chunks, FF_CHUNK)
    wu = w_gu[:, D_FF:].reshape(d, n_chunks, FF_CHUNK)
    wgu = jnp.concatenate([wg, wu], axis=-1).transpose(1, 0, 2).astype(BF16)
    wd = w_down.reshape(n_chunks, FF_CHUNK, d).astype(BF16)
    args = [h, norm_w.reshape(1, d), wgu, wd]
    in_specs = [pl.BlockSpec((tm, d), lambda i: (i, 0)), _const_spec((1, d)),
                _const_spec(wgu.shape), _const_spec(wd.shape)]
    if final_w is not None:
        args.append(final_w.reshape(1, d))
        in_specs.append(_const_spec((1, d)))
    return pl.pallas_call(
        functools.partial(_ffn_kernel, n_chunks=n_chunks, tf=FF_CHUNK, final=final_w is not None),
        grid=(t // tm,),
        in_specs=in_specs,
        out_specs=pl.BlockSpec((tm, d), lambda i: (i, 0)),
        out_shape=jax.ShapeDtypeStruct((t, d), F32),
        scratch_shapes=[pltpu.VMEM((tm, d), BF16), pltpu.VMEM((tm, d), F32)],
        compiler_params=_params("parallel"),
        name="ffn",
    )(*args)


def _proj_kernel(x_ref, nw_ref, *refs, n_out):
    xn = _rms(x_ref[...], nw_ref[...]).astype(BF16)
    for w_ref, o_ref in zip(refs[:n_out], refs[n_out:]):
        o_ref[...] = _dot(xn, w_ref[...]).astype(o_ref.dtype)


def _proj(h, norm_w, weights, out_dtypes, tm=512, name="proj"):
    t, d = h.shape
    n_out = len(weights)
    in_specs = [pl.BlockSpec((tm, d), lambda i: (i, 0)), _const_spec((1, d))]
    in_specs += [_const_spec(w.shape) for w in weights]
    return pl.pallas_call(
        functools.partial(_proj_kernel, n_out=n_out),
        grid=(t // tm,),
        in_specs=in_specs,
        out_specs=[pl.BlockSpec((tm, w.shape[1]), lambda i: (i, 0)) for w in weights],
        out_shape=[jax.ShapeDtypeStruct((t, w.shape[1]), dt) for w, dt in zip(weights, out_dtypes)],
        compiler_params=_params("parallel"),
        name=name,
    )(h, norm_w.reshape(1, d), *weights)


def _out_kernel(a_ref, w_ref, h_ref, o_ref):
    o_ref[...] = h_ref[...] + _dot(a_ref[...], w_ref[...])


def _out_proj(a, w, h, tm=1024):
    t, d = h.shape
    k = a.shape[1]
    return pl.pallas_call(
        _out_kernel,
        grid=(t // tm,),
        in_specs=[pl.BlockSpec((tm, k), lambda i: (i, 0)), _const_spec(w.shape),
                  pl.BlockSpec((tm, d), lambda i: (i, 0))],
        out_specs=pl.BlockSpec((tm, d), lambda i: (i, 0)),
        out_shape=jax.ShapeDtypeStruct((t, d), F32),
        compiler_params=_params("parallel"),
        name="out_proj",
    )(a, w, h)


def _pad_cols(w, n):
    return jnp.pad(w, ((0, 0), (0, n - w.shape[1])))


def _gla_kernel(q_ref, k_ref, v_ref, g_ref, a_ref, wgb_ref, bg_ref, gnw_ref, o_ref, st_ref, *, ts):
    c_len, heads, dk, dv = GLA_CHUNK, GLA_HEADS, GLA_DK_H, GLA_DV_H

    @pl.when(pl.program_id(1) == 0)
    def _():
        st_ref[...] = jnp.zeros_like(st_ref)

    row = lax.broadcasted_iota(jnp.int32, (c_len, c_len), 0)
    col = lax.broadcasted_iota(jnp.int32, (c_len, c_len), 1)
    causal = col <= row
    tril = jnp.where(causal, 1.0, 0.0).astype(BF16)
    scale = dk ** -0.5

    def chunk(c, carry):
        rows = pl.ds(pl.multiple_of(c * c_len, c_len), c_len)
        z = jnp.dot(a_ref[0, rows, :], wgb_ref[...], precision=lax.Precision.HIGHEST,
                    preferred_element_type=F32) + bg_ref[...]
        log_a = (jnp.minimum(z, 0.0) - jnp.log(1.0 + jnp.exp(-jnp.abs(z)))) / GLA_GATE_TEMP
        b = None
        for p in _split_bf16(log_a, 3):
            t = _dot(tril, p)
            b = t if b is None else b + t
        b_last = b[c_len - 1:c_len, :]
        q = q_ref[0, rows, :]
        k = k_ref[0, rows, :]
        q_dec = (q * scale * jnp.exp(b)).astype(BF16)
        k_inv = (k * jnp.exp(-b)).astype(BF16)
        k_end = (k * jnp.exp(b_last - b)).astype(BF16)
        decay = jnp.exp(b_last)
        v = v_ref[0, rows, :].astype(BF16)
        outs = []
        for h in range(heads):
            ks = slice(h * dk, (h + 1) * dk)
            vh = v[:, h * dv:(h + 1) * dv]
            qd = q_dec[:, ks]
            attn = lax.dot_general(qd, k_inv[:, ks], NT_DIMS, preferred_element_type=F32)
            attn = jnp.where(causal, attn, 0.0).astype(BF16)
            st = st_ref[h]
            o = _dot(attn, vh) + lax.dot_general(qd, st.astype(BF16), NT_DIMS,
                                                 preferred_element_type=F32)
            st_ref[h] = st * decay[:, ks] + lax.dot_general(vh, k_end[:, ks], TN_DIMS,
                                                            preferred_element_type=F32)
            outs.append(o * lax.rsqrt(jnp.mean(o * o, axis=-1, keepdims=True) + EPS))
        o = jnp.concatenate(outs, axis=-1)
        o = o * gnw_ref[...] * _silu(g_ref[0, rows, :])
        o_ref[0, rows, :] = o.astype(o_ref.dtype)
        return carry

    lax.fori_loop(0, ts // c_len, chunk, 0)


def _gla_mixer(h, bsz, mix_norm, w_in, w_gate_b, b_gate, gn_w, w_out, ts=512):
    t, d = h.shape
    s = t // bsz
    dk_all, dv_all = GLA_HEADS * GLA_DK_H, GLA_HEADS * GLA_DV_H
    o1, o2, o3, o4 = dk_all, 2 * dk_all, 2 * dk_all + dv_all, 2 * dk_all + 2 * dv_all
    w = w_in.astype(BF16)
    weights = [w[:, :o1], w[:, o1:o2], w[:, o2:o3], w[:, o3:o4], _pad_cols(w[:, o4:], LANES)]
    q, k, v, g, a_lr = _proj(h, mix_norm, weights, [F32] * 5, name="gla_proj")
    wgb = jnp.pad(w_gate_b, ((0, LANES - GLA_GATE_RANK), (0, 0)))

    def seq(x):
        return x.reshape(bsz, s, x.shape[-1])

    def blk(n):
        return pl.BlockSpec((1, ts, n), lambda b, i: (b, i, 0))

    o = pl.pallas_call(
        functools.partial(_gla_kernel, ts=ts),
        grid=(bsz, s // ts),
        in_specs=[blk(dk_all), blk(dk_all), blk(dv_all), blk(dv_all), blk(LANES),
                  _const_spec(wgb.shape), _const_spec((1, dk_all)), _const_spec((1, dv_all))],
        out_specs=blk(dv_all),
        out_shape=jax.ShapeDtypeStruct((bsz, s, dv_all), BF16),
        scratch_shapes=[pltpu.VMEM((GLA_HEADS, GLA_DV_H, GLA_DK_H), F32)],
        compiler_params=_params("parallel", "arbitrary"),
        name="gla_core",
    )(seq(q), seq(k), seq(v), seq(g), seq(a_lr), wgb, b_gate.reshape(1, dk_all), gn_w.reshape(1, dv_all))
    return _out_proj(o.reshape(t, dv_all), w_out.astype(BF16), h)


def _ssd_kernel(z_ref, xbc_ref, dt_ref, cw_ref, cb_ref, dtb_ref, alog_ref, dsk_ref, nw_ref, e_ref,
                o_ref, st_ref, xp_ref, conv_ref, *, ts):
    q_len, n_grp, hpg, p_dim, n_st = SSD_CHUNK, SSD_GROUPS, SSD_HPG, SSD_HEAD_DIM, SSD_STATE
    gw = hpg * p_dim
    si = pl.program_id(1)

    @pl.when(si == 0)
    def _():
        st_ref[...] = jnp.zeros_like(st_ref)
        xp_ref[0:SSD_HALO, :] = jnp.zeros((SSD_HALO, SSD_CONV_DIM), F32)

    @pl.when(si > 0)
    def _():
        xp_ref[0:SSD_HALO, :] = xp_ref[ts:ts + SSD_HALO, :]

    xp_ref[SSD_HALO:SSD_HALO + ts, :] = xbc_ref[0]
    acc = cb_ref[...]
    for kk in range(SSD_CONV):
        off = SSD_HALO - (SSD_CONV - 1) + kk
        acc = acc + cw_ref[kk:kk + 1, :] * xp_ref[off:off + ts, :]
    conv_ref[...] = _silu(acc)

    row = lax.broadcasted_iota(jnp.int32, (q_len, q_len), 0)
    col = lax.broadcasted_iota(jnp.int32, (q_len, q_len), 1)
    causal = col <= row
    tril = jnp.where(causal, 1.0, 0.0).astype(BF16)
    er = lax.broadcasted_iota(jnp.int32, (LANES, LANES), 0)
    ec = lax.broadcasted_iota(jnp.int32, (LANES, LANES), 1)
    eye = jnp.where(er == ec, 1.0, 0.0).astype(BF16)
    lane = lax.broadcasted_iota(jnp.int32, (q_len, LANES), 1)
    low_half = lane < p_dim
    a_neg = -jnp.exp(alog_ref[...])

    def chunk(c, carry):
        rows = pl.ds(pl.multiple_of(c * q_len, q_len), q_len)
        dt = _softplus(dt_ref[0, rows, :] + dtb_ref[...])
        da = dt * a_neg
        a_cs = None
        for p in _split_bf16(da, 3):
            t = _dot(tril, p)
            a_cs = t if a_cs is None else a_cs + t
        a_last = a_cs[q_len - 1:q_len, :]
        a_cs_t = _transpose_via_mxu(eye, a_cs)
        dt_t = _transpose_via_mxu(eye, dt)
        e_mat = e_ref[...]
        ea_x = _dot_exact_rhs(jnp.exp(a_cs), e_mat, 2)
        w_x = _dot_exact_rhs(jnp.exp(a_last - a_cs) * dt, e_mat, 2)
        tot_x = ea_x[q_len - 1:q_len, :]
        for g in range(n_grp):
            gs = slice(g * gw, (g + 1) * gw)
            x_g = conv_ref[rows, gs]
            b_g = conv_ref[rows, SSD_D_INNER + g * n_st:SSD_D_INNER + (g + 1) * n_st].astype(BF16)
            c_g = conv_ref[rows, SSD_D_INNER + (n_grp + g) * n_st:
                           SSD_D_INNER + (n_grp + g + 1) * n_st].astype(BF16)
            cb = lax.dot_general(c_g, b_g, NT_DIMS, preferred_element_type=F32)
            st = st_ref[g]
            y_off = _dot(c_g, st.astype(BF16)) * ea_x[:, gs]
            xw = (x_g * w_x[:, gs]).astype(BF16)
            st_ref[g] = st * tot_x[:, gs] + lax.dot_general(b_g, xw, TN_DIMS, preferred_element_type=F32)
            xb = x_g.astype(BF16)
            ys = []
            for pr in range(hpg // 2):
                h1 = g * hpg + 2 * pr
                x_pair = xb[:, pr * LANES:(pr + 1) * LANES]
                y_pair = None
                for hh, keep in ((h1, low_half), (h1 + 1, jnp.logical_not(low_half))):
                    seg = a_cs[:, hh:hh + 1] - a_cs_t[hh:hh + 1, :]
                    m = cb * jnp.where(causal, jnp.exp(seg), 0.0) * dt_t[hh:hh + 1, :]
                    t = _dot(m.astype(BF16), jnp.where(keep, x_pair, jnp.zeros_like(x_pair)))
                    y_pair = t if y_pair is None else y_pair + t
                ys.append(y_pair)
            y = jnp.concatenate(ys, axis=-1) + y_off + x_g * dsk_ref[:, gs]
            y = y * _silu(z_ref[0, rows, gs])
            y = y * lax.rsqrt(jnp.mean(y * y, axis=-1, keepdims=True) + EPS) * nw_ref[:, gs]
            o_ref[0, rows, gs] = y.astype(o_ref.dtype)
        return carry

    lax.fori_loop(0, ts // q_len, chunk, 0)


def _ssd_mixer(h, bsz, mix_norm, w_in, conv_w, conv_b, dt_bias, a_log, d_skip, norm_w, w_out, ts=256):
    t, d = h.shape
    s = t // bsz
    w = w_in.astype(BF16)
    o1, o2 = SSD_D_INNER, SSD_D_INNER + SSD_CONV_DIM
    weights = [w[:, :o1], w[:, o1:o2], _pad_cols(w[:, o2:], LANES)]
    z, xbc, dt_raw = _proj(h, mix_norm, weights, [F32] * 3, name="ssd_proj")
    head_of_lane = jnp.arange(SSD_D_INNER) // SSD_HEAD_DIM
    expand = (jnp.arange(LANES)[:, None] == head_of_lane[None, :]).astype(BF16)

    def pad_row(v):
        return jnp.pad(v, (0, LANES - v.shape[0])).reshape(1, LANES)

    def seq(x):
        return x.reshape(bsz, s, x.shape[-1])

    def blk(n):
        return pl.BlockSpec((1, ts, n), lambda b, i: (b, i, 0))

    y = pl.pallas_call(
        functools.partial(_ssd_kernel, ts=ts),
        grid=(bsz, s // ts),
        in_specs=[blk(SSD_D_INNER), blk(SSD_CONV_DIM), blk(LANES),
                  _const_spec((SSD_CONV, SSD_CONV_DIM)), _const_spec((1, SSD_CONV_DIM)),
                  _const_spec((1, LANES)), _const_spec((1, LANES)),
                  _const_spec((1, SSD_D_INNER)), _const_spec((1, SSD_D_INNER)),
                  _const_spec((LANES, SSD_D_INNER))],
        out_specs=blk(SSD_D_INNER),
        out_shape=jax.ShapeDtypeStruct((bsz, s, SSD_D_INNER), BF16),
        scratch_shapes=[pltpu.VMEM((SSD_GROUPS, SSD_STATE, SSD_HPG * SSD_HEAD_DIM), F32),
                        pltpu.VMEM((ts + SSD_HALO, SSD_CONV_DIM), F32),
                        pltpu.VMEM((ts, SSD_CONV_DIM), F32)],
        compiler_params=_params("parallel", "arbitrary"),
        name="ssd_core",
    )(seq(z), seq(xbc), seq(dt_raw), conv_w, conv_b.reshape(1, -1), pad_row(dt_bias), pad_row(a_log),
      jnp.repeat(d_skip, SSD_HEAD_DIM).reshape(1, -1), norm_w.reshape(1, -1), expand)
    return _out_proj(y.reshape(t, SSD_D_INNER), w_out.astype(BF16), h)


def _mla_qkv_kernel(cq_ref, ckv_ref, kr_ref, pos_ref, qn_ref, kvn_ref, wuq_ref, wuk_ref, wuv_ref, freq_ref,
                    q_ref, k_ref, v_ref):
    half = MLA_ROPE // 2
    cqn = _rms(cq_ref[...], qn_ref[...]).astype(BF16)
    ckvn = _rms(ckv_ref[...], kvn_ref[...]).astype(BF16)
    ang = pos_ref[...] * freq_ref[...]
    lane = lax.broadcasted_iota(jnp.int32, ang.shape, 1)
    cos_v = jnp.where(lane < MLA_ROPE, jnp.cos(ang), 0.0)
    sin_a = jnp.sin(ang)
    sin_v = jnp.where(lane < half, -sin_a, jnp.where(lane < MLA_ROPE, sin_a, 0.0))

    def rope(t):
        return t * cos_v + pltpu.roll(t, MLA_ROPE, 1) * sin_v

    scale = MLA_QK ** -0.5
    q = _dot(cqn, wuq_ref[...])
    k_rope = rope(kr_ref[...]).astype(BF16)
    k_nope = _dot(ckvn, wuk_ref[...])
    for h in range(MLA_HEADS):
        base = h * MLA_QK_PAD
        q_ref[:, base:base + MLA_NOPE] = (q[:, base:base + MLA_NOPE] * scale).astype(BF16)
        q_ref[:, base + MLA_NOPE:base + MLA_QK_PAD] = (
            rope(q[:, base + MLA_NOPE:base + MLA_QK_PAD]) * scale).astype(BF16)
        k_ref[:, base:base + MLA_NOPE] = k_nope[:, h * MLA_NOPE:(h + 1) * MLA_NOPE].astype(BF16)
        k_ref[:, base + MLA_NOPE:base + MLA_QK_PAD] = k_rope
    v_ref[...] = _dot(ckvn, wuv_ref[...]).astype(BF16)


def _flash_kernel(q_ref, k_ref, v_ref, o_ref, *, tq):
    qi = pl.program_id(2)
    q = q_ref[0]
    row = lax.broadcasted_iota(jnp.int32, (tq, tq), 0)
    col = lax.broadcasted_iota(jnp.int32, (tq, tq), 1)

    def step(j, carry, masked):
        m, l, acc = carry
        ks = pl.ds(pl.multiple_of(j * tq, tq), tq)
        s = lax.dot_general(q, k_ref[0, ks, :], NT_DIMS, preferred_element_type=F32)
        if masked:
            s = jnp.where(col <= row, s, -jnp.inf)
        m_new = jnp.maximum(m, jnp.max(s, axis=-1, keepdims=True))
        p = jnp.exp(s - m_new)
        alpha = jnp.exp(m - m_new)
        l = alpha * l + jnp.sum(p, axis=-1, keepdims=True)
        acc = alpha * acc + _dot(p.astype(BF16), v_ref[0, ks, :])
        return m_new, l, acc

    init = (jnp.full((tq, 1), -jnp.inf, F32), jnp.zeros((tq, 1), F32), jnp.zeros((tq, MLA_V), F32))
    carry = lax.fori_loop(0, qi, lambda j, cr: step(j, cr, False), init)
    _, l, acc = step(qi, carry, True)
    o_ref[0] = (acc / l).astype(o_ref.dtype)


def _mla_mixer(h, bsz, positions, mix_norm, w_in, q_norm, w_uq, kv_norm, w_ukv, w_out, tm=512, tq=512):
    t, d = h.shape
    s = t // bsz
    half = MLA_ROPE // 2

    def swap_halves(w):
        return jnp.concatenate([w[..., half:], w[..., :half]], axis=-1)

    o1, o2 = MLA_Q_LORA, MLA_Q_LORA + MLA_KV_LORA
    w_kr = w_in[:, o2:]
    weights = [w_in[:, :o1].astype(BF16), w_in[:, o1:o2].astype(BF16),
               jnp.concatenate([w_kr, swap_halves(w_kr)], axis=-1).astype(BF16)]
    cq, ckv, kr = _proj(h, mix_norm, weights, [F32] * 3, name="mla_proj")

    wq = w_uq.reshape(MLA_Q_LORA, MLA_HEADS, MLA_QK)
    wq_rope = wq[..., MLA_NOPE:]
    wuq = jnp.concatenate([wq[..., :MLA_NOPE], wq_rope, swap_halves(wq_rope)], axis=-1)
    wuq = wuq.reshape(MLA_Q_LORA, MLA_HEADS * MLA_QK_PAD).astype(BF16)
    wkv = w_ukv.reshape(MLA_KV_LORA, MLA_HEADS, MLA_NOPE + MLA_V)
    wuk = wkv[..., :MLA_NOPE].reshape(MLA_KV_LORA, MLA_HEADS * MLA_NOPE).astype(BF16)
    wuv = wkv[..., MLA_NOPE:].reshape(MLA_KV_LORA, MLA_HEADS * MLA_V).astype(BF16)
    inv_freq = 1.0 / (ROPE_THETA ** (jnp.arange(0, MLA_ROPE, 2, dtype=F32) / MLA_ROPE))
    freq = jnp.concatenate([inv_freq, inv_freq, jnp.zeros((LANES - MLA_ROPE,), F32)]).reshape(1, LANES)
    pos = positions.astype(F32).reshape(t, 1)

    def rows(n):
        return pl.BlockSpec((tm, n), lambda i: (i, 0))

    qk_w = MLA_HEADS * MLA_QK_PAD
    v_w = MLA_HEADS * MLA_V
    qf, kf, vf = pl.pallas_call(
        _mla_qkv_kernel,
        grid=(t // tm,),
        in_specs=[rows(MLA_Q_LORA), rows(MLA_KV_LORA), rows(LANES), rows(1),
                  _const_spec((1, MLA_Q_LORA)), _const_spec((1, MLA_KV_LORA)),
                  _const_spec(wuq.shape), _const_spec(wuk.shape), _const_spec(wuv.shape),
                  _const_spec((1, LANES))],
        out_specs=[rows(qk_w), rows(qk_w), rows(v_w)],
        out_shape=[jax.ShapeDtypeStruct((t, qk_w), BF16), jax.ShapeDtypeStruct((t, qk_w), BF16),
                   jax.ShapeDtypeStruct((t, v_w), BF16)],
        compiler_params=_params("parallel"),
        name="mla_qkv",
    )(cq, ckv, kr, pos, q_norm.reshape(1, -1), kv_norm.reshape(1, -1), wuq, wuk, wuv, freq)

    o = pl.pallas_call(
        functools.partial(_flash_kernel, tq=tq),
        grid=(bsz, MLA_HEADS, s // tq),
        in_specs=[pl.BlockSpec((1, tq, MLA_QK_PAD), lambda b, hd, i: (b, i, hd)),
                  pl.BlockSpec((1, s, MLA_QK_PAD), lambda b, hd, i: (b, 0, hd)),
                  pl.BlockSpec((1, s, MLA_V), lambda b, hd, i: (b, 0, hd))],
        out_specs=pl.BlockSpec((1, tq, MLA_V), lambda b, hd, i: (b, i, hd)),
        out_shape=jax.ShapeDtypeStruct((bsz, s, v_w), BF16),
        compiler_params=_params("parallel", "parallel", "arbitrary"),
        name="mla_flash",
    )(qf.reshape(bsz, s, qk_w), kf.reshape(bsz, s, qk_w), vf.reshape(bsz, s, v_w))
    return _out_proj(o.reshape(t, v_w), w_out.astype(BF16), h)


def kernel(x, positions, l0_ffn1_norm, l0_ffn1_w_gu, l0_ffn1_w_down, l0_mix_norm, l0_gla_w_in, l0_gla_w_gate_b, l0_gla_b_gate, l0_gla_norm, l0_gla_w_out, l0_ffn2_norm, l0_ffn2_w_gu, l0_ffn2_w_down, l1_ffn1_norm, l1_ffn1_w_gu, l1_ffn1_w_down, l1_mix_norm, l1_ssd_w_in, l1_ssd_conv_w, l1_ssd_conv_b, l1_ssd_dt_bias, l1_ssd_a_log, l1_ssd_d_skip, l1_ssd_norm, l1_ssd_w_out, l1_ffn2_norm, l1_ffn2_w_gu, l1_ffn2_w_down, l2_ffn1_norm, l2_ffn1_w_gu, l2_ffn1_w_down, l2_mix_norm, l2_mla_w_in, l2_mla_q_norm, l2_mla_w_uq, l2_mla_kv_norm, l2_mla_w_ukv, l2_mla_w_out, l2_ffn2_norm, l2_ffn2_w_gu, l2_ffn2_w_down, l3_ffn1_norm, l3_ffn1_w_gu, l3_ffn1_w_down, l3_mix_norm, l3_gla_w_in, l3_gla_w_gate_b, l3_gla_b_gate, l3_gla_norm, l3_gla_w_out, l3_ffn2_norm, l3_ffn2_w_gu, l3_ffn2_w_down, final_norm):
    bsz, s, d = x.shape
    h = x.reshape(bsz * s, d)

    h = _ffn(h, l0_ffn1_norm, l0_ffn1_w_gu, l0_ffn1_w_down)
    h = _gla_mixer(h, bsz, l0_mix_norm, l0_gla_w_in, l0_gla_w_gate_b, l0_gla_b_gate, l0_gla_norm, l0_gla_w_out)
    h = _ffn(h, l0_ffn2_norm, l0_ffn2_w_gu, l0_ffn2_w_down)

    h = _ffn(h, l1_ffn1_norm, l1_ffn1_w_gu, l1_ffn1_w_down)
    h = _ssd_mixer(h, bsz, l1_mix_norm, l1_ssd_w_in, l1_ssd_conv_w, l1_ssd_conv_b, l1_ssd_dt_bias,
                   l1_ssd_a_log, l1_ssd_d_skip, l1_ssd_norm, l1_ssd_w_out)
    h = _ffn(h, l1_ffn2_norm, l1_ffn2_w_gu, l1_ffn2_w_down)

    h = _ffn(h, l2_ffn1_norm, l2_ffn1_w_gu, l2_ffn1_w_down)
    h = _mla_mixer(h, bsz, positions, l2_mix_norm, l2_mla_w_in, l2_mla_q_norm, l2_mla_w_uq,
                   l2_mla_kv_norm, l2_mla_w_ukv, l2_mla_w_out)
    h = _ffn(h, l2_ffn2_norm, l2_ffn2_w_gu, l2_ffn2_w_down)

    h = _ffn(h, l3_ffn1_norm, l3_ffn1_w_gu, l3_ffn1_w_down)
    h = _gla_mixer(h, bsz, l3_mix_norm, l3_gla_w_in, l3_gla_w_gate_b, l3_gla_b_gate, l3_gla_norm, l3_gla_w_out)
    h = _ffn(h, l3_ffn2_norm, l3_ffn2_w_gu, l3_ffn2_w_down, final_w=final_norm)
    return h.reshape(bsz, s, d)
```

```python
import functools
import math

import jax
import jax.numpy as jnp
from jax import lax
from jax.experimental import pallas as pl
from jax.experimental.pallas import tpu as pltpu

F32 = jnp.float32
BF16 = jnp.bfloat16
EPS = 1e-6

V7X_VMEM_BYTES = 64 * 1024 * 1024
VMEM_LIMIT = V7X_VMEM_BYTES - 8 * 1024 * 1024
LANES = 128

D_FF = 2816
FF_CHUNK = 256

GLA_HEADS = 4
GLA_DK_H = 128
GLA_DV_H = 256
GLA_GATE_RANK = 16
GLA_GATE_TEMP = 16.0
GLA_CHUNK = 64

SSD_D_INNER = 2048
SSD_HEAD_DIM = 64
SSD_HEADS = 32
SSD_GROUPS = 4
SSD_HPG = 8
SSD_STATE = 128
SSD_CONV = 4
SSD_CHUNK = 128
SSD_CONV_DIM = 3072
SSD_HALO = 8

MLA_HEADS = 8
MLA_Q_LORA = 384
MLA_KV_LORA = 256
MLA_NOPE = 128
MLA_ROPE = 64
MLA_V = 128
MLA_QK = MLA_NOPE + MLA_ROPE
MLA_QK_PAD = 256
ROPE_THETA = 10000.0
MLA_V_PAD = 256
FLASH_HEADS_PER_STEP = 4

NT_DIMS = (((1,), (1,)), ((), ()))
TN_DIMS = (((0,), (0,)), ((), ()))


def _params(*sem):
    return pltpu.CompilerParams(dimension_semantics=sem, vmem_limit_bytes=VMEM_LIMIT)


def _const_spec(shape):
    return pl.BlockSpec(shape, lambda *_: (0,) * len(shape), pipeline_mode=pl.Buffered(1))


def _rms(x, w):
    return x * lax.rsqrt(jnp.mean(x * x, axis=-1, keepdims=True) + EPS) * w


def _silu(x):
    return x / (1.0 + jnp.exp(-x))


def _softplus(x):
    return jnp.maximum(x, 0.0) + jnp.log(1.0 + jnp.exp(-jnp.abs(x)))


def _dot(a, b):
    return jnp.dot(a, b, preferred_element_type=F32)


def _split_bf16(a, terms):
    parts = []
    for _ in range(terms):
        p = a.astype(BF16)
        parts.append(p)
        a = a - p.astype(F32)
    return parts


def _dot_exact_rhs(a, b, terms, dims=None):
    out = None
    for p in _split_bf16(a, terms):
        if dims is None:
            t = _dot(p, b)
        else:
            t = lax.dot_general(p, b, dims, preferred_element_type=F32)
        out = t if out is None else out + t
    return out


def _transpose_via_mxu(eye, a, terms=3):
    out = None
    for p in _split_bf16(a, terms):
        t = lax.dot_general(eye, p, NT_DIMS, preferred_element_type=F32)
        out = t if out is None else out + t
    return out


def _ffn_kernel(*refs, n_chunks, tf, final):
    if final:
        x_ref, nw_ref, wgu_ref, wd_ref, fw_ref, o_ref, xn_ref, acc_ref = refs
    else:
        x_ref, nw_ref, wgu_ref, wd_ref, o_ref, xn_ref, acc_ref = refs
    d_ff = n_chunks * tf
    xn_ref[...] = _rms(x_ref[...], nw_ref[...]).astype(BF16)

    def down(c):
        if isinstance(c, int):
            g_cols, u_cols = pl.ds(c * tf, tf), pl.ds(d_ff + c * tf, tf)
        else:
            g_cols = pl.ds(pl.multiple_of(c * tf, tf), tf)
            u_cols = pl.ds(pl.multiple_of(d_ff + c * tf, tf), tf)
        xn = xn_ref[...]
        a = (_silu(_dot(xn, wgu_ref[:, g_cols])) * _dot(xn, wgu_ref[:, u_cols])).astype(BF16)
        return _dot(a, wd_ref[g_cols, :])

    acc_ref[...] = down(0)

    for c in range(1, n_chunks - 1):
        acc_ref[...] += down(c)
    y = x_ref[...] + (acc_ref[...] + down(n_chunks - 1))
    if final:
        y = _rms(y, fw_ref[...])
    o_ref[...] = y


def _ffn(h, norm_w, w_gu, w_down, final_w=None, tm=1024):
    t, d = h.shape
    n_chunks = D_FF // FF_CHUNK
    wgu = w_gu.astype(BF16)
    wd = (0.5 * w_down).astype(BF16)
    args = [h, norm_w.reshape(1, d), wgu, wd]
    in_specs = [pl.BlockSpec((tm, d), lambda i: (i, 0)), _const_spec((1, d)),
                _const_spec(wgu.shape), _const_spec(wd.shape)]
    if final_w is not None:
        args.append(final_w.reshape(1, d))
        in_specs.append(_const_spec((1, d)))
    return pl.pallas_call(
        functools.partial(_ffn_kernel, n_chunks=n_chunks, tf=FF_CHUNK, final=final_w is not None),
        grid=(t // tm,),
        in_specs=in_specs,
        out_specs=pl.BlockSpec((tm, d), lambda i: (i, 0)),
        out_shape=jax.ShapeDtypeStruct((t, d), F32),
        scratch_shapes=[pltpu.VMEM((tm, d), BF16), pltpu.VMEM((tm, d), F32)],
        compiler_params=_params("parallel"),
        name="ffn",
    )(*args)


def _proj_kernel(x_ref, nw_ref, *refs, n_out):
    xn = _rms(x_ref[...], nw_ref[...]).astype(BF16)
    for w_ref, o_ref in zip(refs[:n_out], refs[n_out:]):
        o_ref[...] = _dot(xn, w_ref[...]).astype(o_ref.dtype)


def _proj(h, norm_w, weights, out_dtypes, tm=512, name="proj"):
    t, d = h.shape
    n_out = len(weights)
    in_specs = [pl.BlockSpec((tm, d), lambda i: (i, 0)), _const_spec((1, d))]
    in_specs += [_const_spec(w.shape) for w in weights]
    return pl.pallas_call(
        functools.partial(_proj_kernel, n_out=n_out),
        grid=(t // tm,),
        in_specs=in_specs,
        out_specs=[pl.BlockSpec((tm, w.shape[1]), lambda i: (i, 0)) for w in weights],
        out_shape=[jax.ShapeDtypeStruct((t, w.shape[1]), dt) for w, dt in zip(weights, out_dtypes)],
        compiler_params=_params("parallel"),
        name=name,
    )(h, norm_w.reshape(1, d), *weights)


def _out_kernel(a_ref, w_ref, h_ref, o_ref):
    o_ref[...] = h_ref[...] + _dot(a_ref[...], w_ref[...])


def _out_proj(a, w, h, tm=1024):
    t, d = h.shape
    k = a.shape[1]
    return pl.pallas_call(
        _out_kernel,
        grid=(t // tm,),
        in_specs=[pl.BlockSpec((tm, k), lambda i: (i, 0)), _const_spec(w.shape),
                  pl.BlockSpec((tm, d), lambda i: (i, 0))],
        out_specs=pl.BlockSpec((tm, d), lambda i: (i, 0)),
        out_shape=jax.ShapeDtypeStruct((t, d), F32),
        compiler_params=_params("parallel"),
        name="out_proj",
    )(a, w, h)


def _pad_cols(w, n):
    return jnp.pad(w, ((0, 0), (0, n - w.shape[1])))


def _gla_kernel(q_ref, k_ref, v_ref, g_ref, a_ref, wgb_ref, bg_ref, gnw_ref, o_ref, st_ref, la_ref, *, ts):
    c_len, heads, dk, dv = GLA_CHUNK, GLA_HEADS, GLA_DK_H, GLA_DV_H

    @pl.when(pl.program_id(1) == 0)
    def _():
        st_ref[...] = jnp.zeros_like(st_ref)

    row = lax.broadcasted_iota(jnp.int32, (c_len, c_len), 0)
    col = lax.broadcasted_iota(jnp.int32, (c_len, c_len), 1)
    causal = col <= row
    tril = jnp.where(causal, 1.0, 0.0).astype(BF16)
    scale = dk ** -0.5

    a_hi, a_lo = _split_bf16(a_ref[0], 2)
    z = _dot(a_hi, wgb_ref[0]) + (_dot(a_hi, wgb_ref[1]) + _dot(a_lo, wgb_ref[0])) + bg_ref[...]
    la_ref[...] = (jnp.minimum(z, 0.0) - jnp.log(1.0 + jnp.exp(-jnp.abs(z)))) / GLA_GATE_TEMP

    def chunk(c):
        rows = pl.ds(c * c_len, c_len)
        b = None
        for p in _split_bf16(la_ref[rows, :], 3):
            t = _dot(tril, p)
            b = t if b is None else b + t
        b_last = b[c_len - 1:c_len, :]
        q = q_ref[0, rows, :]
        k = k_ref[0, rows, :]
        q_dec = (q * scale * jnp.exp(b)).astype(BF16)
        k_inv = (k * jnp.exp(-b)).astype(BF16)
        k_end = (k * jnp.exp(b_last - b)).astype(BF16)
        decay = jnp.exp(b_last)
        v = v_ref[0, rows, :].astype(BF16)
        outs = []
        for h in range(heads):
            ks = slice(h * dk, (h + 1) * dk)
            vh = v[:, h * dv:(h + 1) * dv]
            qd = q_dec[:, ks]
            attn = lax.dot_general(qd, k_inv[:, ks], NT_DIMS, preferred_element_type=F32)
            attn = jnp.where(causal, attn, 0.0).astype(BF16)
            st = st_ref[h]
            o = _dot(attn, vh) + lax.dot_general(qd, st.astype(BF16), NT_DIMS,
                                                 preferred_element_type=F32)
            st_ref[h] = st * decay[:, ks] + lax.dot_general(vh, k_end[:, ks], TN_DIMS,
                                                            preferred_element_type=F32)
            outs.append(o * lax.rsqrt(jnp.mean(o * o, axis=-1, keepdims=True) + EPS))
        o = jnp.concatenate(outs, axis=-1)
        o = o * gnw_ref[...] * _silu(g_ref[0, rows, :])
        o_ref[0, rows, :] = o.astype(o_ref.dtype)

    for c in range(ts // c_len):
        chunk(c)


def _gla_mixer(h, bsz, mix_norm, w_in, w_gate_b, b_gate, gn_w, w_out, ts=512):
    t, d = h.shape
    s = t // bsz
    dk_all, dv_all = GLA_HEADS * GLA_DK_H, GLA_HEADS * GLA_DV_H
    o1, o2, o3, o4 = dk_all, 2 * dk_all, 2 * dk_all + dv_all, 2 * dk_all + 2 * dv_all
    w = w_in.astype(BF16)
    weights = [w[:, :o1], w[:, o1:o2], w[:, o2:o3], w[:, o3:o4], _pad_cols(w[:, o4:], LANES)]
    q, k, v, g, a_lr = _proj(h, mix_norm, weights, [F32] * 5, name="gla_proj")
    wgb = jnp.pad(w_gate_b, ((0, LANES - GLA_GATE_RANK), (0, 0)))
    wgb_hi = wgb.astype(BF16)
    wgb = jnp.stack([wgb_hi, (wgb - wgb_hi.astype(F32)).astype(BF16)])

    def seq(x):
        return x.reshape(bsz, s, x.shape[-1])

    def blk(n):
        return pl.BlockSpec((1, ts, n), lambda b, i: (b, i, 0))

    o = pl.pallas_call(
        functools.partial(_gla_kernel, ts=ts),
        grid=(bsz, s // ts),
        in_specs=[blk(dk_all), blk(dk_all), blk(dv_all), blk(dv_all), blk(LANES),
                  _const_spec(wgb.shape), _const_spec((1, dk_all)), _const_spec((1, dv_all))],
        out_specs=blk(dv_all),
        out_shape=jax.ShapeDtypeStruct((bsz, s, dv_all), BF16),
        scratch_shapes=[pltpu.VMEM((GLA_HEADS, GLA_DV_H, GLA_DK_H), F32), pltpu.VMEM((ts, dk_all), F32)],
        compiler_params=_params("parallel", "arbitrary"),
        name="gla_core",
    )(seq(q), seq(k), seq(v), seq(g), seq(a_lr), wgb, b_gate.reshape(1, dk_all), gn_w.reshape(1, dv_all))
    return _out_proj(o.reshape(t, dv_all), w_out.astype(BF16), h)


def _ssd_kernel(z_ref, xbc_ref, dt_ref, cw_ref, cb_ref, dtb_ref, alog_ref, dsk_ref, nw_ref, e_ref,
                o_ref, st_ref, xp_ref, conv_ref, *, ts):
    q_len, n_grp, hpg, p_dim, n_st = SSD_CHUNK, SSD_GROUPS, SSD_HPG, SSD_HEAD_DIM, SSD_STATE
    gw = hpg * p_dim
    si = pl.program_id(1)

    @pl.when(si == 0)
    def _():
        st_ref[...] = jnp.zeros_like(st_ref)
        xp_ref[0:SSD_HALO, :] = jnp.zeros((SSD_HALO, SSD_CONV_DIM), F32)

    @pl.when(si > 0)
    def _():
        xp_ref[0:SSD_HALO, :] = xp_ref[ts:ts + SSD_HALO, :]

    xp_ref[SSD_HALO:SSD_HALO + ts, :] = xbc_ref[0]
    tiles = xp_ref[...].reshape((ts + SSD_HALO) // SSD_HALO, SSD_HALO, SSD_CONV_DIM)
    sub = lax.broadcasted_iota(jnp.int32, (ts // SSD_HALO, SSD_HALO, SSD_CONV_DIM), 1)
    acc = cb_ref[...] + cw_ref[SSD_CONV - 1:SSD_CONV, :] * tiles[1:]
    for delay in range(1, SSD_CONV):
        rot = pltpu.roll(tiles, delay, 1)
        acc = acc + cw_ref[SSD_CONV - 1 - delay:SSD_CONV - delay, :] * jnp.where(sub >= delay, rot[1:], rot[:-1])
    conv_ref[...] = _silu(acc).reshape(ts, SSD_CONV_DIM)

    row = lax.broadcasted_iota(jnp.int32, (q_len, q_len), 0)
    col = lax.broadcasted_iota(jnp.int32, (q_len, q_len), 1)
    causal = col <= row
    tril = jnp.where(causal, 1.0, 0.0).astype(BF16)
    er = lax.broadcasted_iota(jnp.int32, (LANES, LANES), 0)
    ec = lax.broadcasted_iota(jnp.int32, (LANES, LANES), 1)
    eye = jnp.where(er == ec, 1.0, 0.0).astype(BF16)
    lane = lax.broadcasted_iota(jnp.int32, (q_len, LANES), 1)
    low_half = lane < p_dim
    a_neg = -jnp.exp(alog_ref[...])

    def chunk(c):
        rows = pl.ds(c * q_len, q_len)
        dt = _softplus(dt_ref[0, rows, :] + dtb_ref[...])
        da = dt * a_neg
        a_cs = None
        for p in _split_bf16(da, 3):
            t = _dot(tril, p)
            a_cs = t if a_cs is None else a_cs + t
        a_last = a_cs[q_len - 1:q_len, :]
        a_cs_t = _transpose_via_mxu(eye, a_cs)
        shift_t = a_cs_t - jnp.log(_transpose_via_mxu(eye, dt))
        e_mat = e_ref[...]
        ea_x = _dot_exact_rhs(jnp.exp(a_cs), e_mat, 2)
        w_x = _dot_exact_rhs(jnp.exp(a_last - a_cs) * dt, e_mat, 2)
        tot_x = ea_x[q_len - 1:q_len, :]
        for g in range(n_grp):
            gs = slice(g * gw, (g + 1) * gw)
            x_g = conv_ref[rows, gs]
            b_g = conv_ref[rows, SSD_D_INNER + g * n_st:SSD_D_INNER + (g + 1) * n_st].astype(BF16)
            c_g = conv_ref[rows, SSD_D_INNER + (n_grp + g) * n_st:
                           SSD_D_INNER + (n_grp + g + 1) * n_st].astype(BF16)
            cb = lax.dot_general(c_g, b_g, NT_DIMS, preferred_element_type=F32)
            st = st_ref[g]
            y_off = _dot(c_g, st.astype(BF16)) * ea_x[:, gs]
            xw = (x_g * w_x[:, gs]).astype(BF16)
            st_ref[g] = st * tot_x[:, gs] + lax.dot_general(b_g, xw, TN_DIMS, preferred_element_type=F32)
            xb = x_g.astype(BF16)
            ys = []
            for pr in range(hpg // 2):
                h1 = g * hpg + 2 * pr
                x_pair = xb[:, pr * LANES:(pr + 1) * LANES]
                y_pair = None
                for hh, keep in ((h1, low_half), (h1 + 1, jnp.logical_not(low_half))):
                    seg = a_cs[:, hh:hh + 1] - shift_t[hh:hh + 1, :]
                    m = cb * jnp.where(causal, jnp.exp(seg), 0.0)
                    t = _dot(m.astype(BF16), jnp.where(keep, x_pair, jnp.zeros_like(x_pair)))
                    y_pair = t if y_pair is None else y_pair + t
                ys.append(y_pair)
            y = jnp.concatenate(ys, axis=-1) + y_off + x_g * dsk_ref[:, gs]
            y = y * _silu(z_ref[0, rows, gs])
            y = y * lax.rsqrt(jnp.mean(y * y, axis=-1, keepdims=True) + EPS) * nw_ref[:, gs]
            o_ref[0, rows, gs] = y.astype(o_ref.dtype)

    for c in range(ts // q_len):
        chunk(c)


def _ssd_mixer(h, bsz, mix_norm, w_in, conv_w, conv_b, dt_bias, a_log, d_skip, norm_w, w_out, ts=256):
    t, d = h.shape
    s = t // bsz
    w = w_in.astype(BF16)
    o1, o2 = SSD_D_INNER, SSD_D_INNER + SSD_CONV_DIM
    weights = [w[:, :o1], w[:, o1:o2], _pad_cols(w[:, o2:], LANES)]
    z, xbc, dt_raw = _proj(h, mix_norm, weights, [F32] * 3, name="ssd_proj")
    head_of_lane = jnp.arange(SSD_D_INNER) // SSD_HEAD_DIM
    expand = (jnp.arange(LANES)[:, None] == head_of_lane[None, :]).astype(BF16)

    def pad_row(v):
        return jnp.pad(v, (0, LANES - v.shape[0])).reshape(1, LANES)

    def seq(x):
        return x.reshape(bsz, s, x.shape[-1])

    def blk(n):
        return pl.BlockSpec((1, ts, n), lambda b, i: (b, i, 0))

    y = pl.pallas_call(
        functools.partial(_ssd_kernel, ts=ts),
        grid=(bsz, s // ts),
        in_specs=[blk(SSD_D_INNER), blk(SSD_CONV_DIM), blk(LANES),
                  _const_spec((SSD_CONV, SSD_CONV_DIM)), _const_spec((1, SSD_CONV_DIM)),
                  _const_spec((1, LANES)), _const_spec((1, LANES)),
                  _const_spec((1, SSD_D_INNER)), _const_spec((1, SSD_D_INNER)),
                  _const_spec((LANES, SSD_D_INNER))],
        out_specs=blk(SSD_D_INNER),
        out_shape=jax.ShapeDtypeStruct((bsz, s, SSD_D_INNER), BF16),
        scratch_shapes=[pltpu.VMEM((SSD_GROUPS, SSD_STATE, SSD_HPG * SSD_HEAD_DIM), F32),
                        pltpu.VMEM((ts + SSD_HALO, SSD_CONV_DIM), F32),
                        pltpu.VMEM((ts, SSD_CONV_DIM), F32)],
        compiler_params=_params("parallel", "arbitrary"),
        name="ssd_core",
    )(seq(z), seq(xbc), seq(dt_raw), conv_w, conv_b.reshape(1, -1), pad_row(dt_bias), pad_row(a_log),
      jnp.repeat(d_skip, SSD_HEAD_DIM).reshape(1, -1), norm_w.reshape(1, -1), expand)
    return _out_proj(y.reshape(t, SSD_D_INNER), w_out.astype(BF16), h)


def _mla_qkv_kernel(cq_ref, ckv_ref, kr_ref, pos_ref, qn_ref, kvn_ref, wuq_ref, wuk_ref, wuv_ref, freq_ref,
                    q_ref, k_ref, v_ref):
    half = MLA_ROPE // 2
    cqn = _rms(cq_ref[...], qn_ref[...]).astype(BF16)
    ckvn = _rms(ckv_ref[...], kvn_ref[...]).astype(BF16)
    ang = pos_ref[...] * freq_ref[...]
    lane = lax.broadcasted_iota(jnp.int32, ang.shape, 1)
    cos_v = jnp.where(lane < MLA_ROPE, jnp.cos(ang), 0.0)
    sin_a = jnp.sin(ang)
    sin_v = jnp.where(lane < half, -sin_a, jnp.where(lane < MLA_ROPE, sin_a, 0.0))

    def rope(t):
        return t * cos_v + pltpu.roll(t, MLA_ROPE, 1) * sin_v

    scale = MLA_QK ** -0.5 * math.log2(math.e)
    q = _dot(cqn, wuq_ref[...])
    k_rope = rope(kr_ref[...]).astype(BF16)
    k_nope = _dot(ckvn, wuk_ref[...])
    for h in range(MLA_HEADS):
        base = h * MLA_QK_PAD
        q_ref[:, base:base + MLA_NOPE] = (q[:, base:base + MLA_NOPE] * scale).astype(BF16)
        q_ref[:, base + MLA_NOPE:base + MLA_QK_PAD] = (
            rope(q[:, base + MLA_NOPE:base + MLA_QK_PAD]) * scale).astype(BF16)
        k_ref[:, base:base + MLA_NOPE] = k_nope[:, h * MLA_NOPE:(h + 1) * MLA_NOPE].astype(BF16)
        k_ref[:, base + MLA_NOPE:base + MLA_QK_PAD] = k_rope
    v = _dot(ckvn, wuv_ref[...])
    ones_col = jnp.where(lane == 0, 1.0, 0.0).astype(BF16)
    for h in range(MLA_HEADS):
        base = h * MLA_V_PAD
        v_ref[:, base:base + MLA_V] = v[:, h * MLA_V:(h + 1) * MLA_V].astype(BF16)
        v_ref[:, base + MLA_V:base + MLA_V_PAD] = ones_col


def _flash_kernel(q_ref, k_ref, v_ref, o_ref, acc_ref, *, tq, hps):
    qi = pl.program_id(2)
    acc_ref[...] = jnp.zeros_like(acc_ref)
    row = lax.broadcasted_iota(jnp.int32, (tq, tq), 0)
    col = lax.broadcasted_iota(jnp.int32, (tq, tq), 1)

    def step(j, ms, masked):
        ks = pl.ds(pl.multiple_of(j * tq, tq), tq)
        new_ms = []
        for hd in range(hps):
            q = q_ref[0, :, hd * MLA_QK_PAD:(hd + 1) * MLA_QK_PAD]
            k = k_ref[0, ks, hd * MLA_QK_PAD:(hd + 1) * MLA_QK_PAD]
            s = lax.dot_general(q, k, NT_DIMS, preferred_element_type=F32)
            if masked:
                s = jnp.where(col <= row, s, -jnp.inf)
            m_new = jnp.maximum(ms[hd], jnp.max(s, axis=-1, keepdims=True))
            p = jnp.exp2((s - m_new).astype(BF16))
            v = v_ref[0, ks, hd * MLA_V_PAD:(hd + 1) * MLA_V_PAD]
            acc_ref[hd] = jnp.exp2(ms[hd] - m_new) * acc_ref[hd] + _dot(p, v)
            new_ms.append(m_new)
        return tuple(new_ms)

    init = tuple(jnp.full((tq, 1), -jnp.inf, F32) for _ in range(hps))
    ms = lax.fori_loop(0, qi, lambda j, cr: step(j, cr, False), init)
    step(qi, ms, True)
    for hd in range(hps):
        acc = acc_ref[hd]
        o_ref[0, :, hd * MLA_V:(hd + 1) * MLA_V] = (acc[:, :MLA_V] / acc[:, MLA_V:MLA_V + 1]).astype(o_ref.dtype)


def _mla_mixer(h, bsz, positions, mix_norm, w_in, q_norm, w_uq, kv_norm, w_ukv, w_out, tm=512, tq=512):
    t, d = h.shape
    s = t // bsz
    half = MLA_ROPE // 2

    def swap_halves(w):
        return jnp.concatenate([w[..., half:], w[..., :half]], axis=-1)

    o1, o2 = MLA_Q_LORA, MLA_Q_LORA + MLA_KV_LORA
    w_kr = w_in[:, o2:]
    weights = [w_in[:, :o1].astype(BF16), w_in[:, o1:o2].astype(BF16),
               jnp.concatenate([w_kr, swap_halves(w_kr)], axis=-1).astype(BF16)]
    cq, ckv, kr = _proj(h, mix_norm, weights, [F32] * 3, name="mla_proj")

    wq = w_uq.reshape(MLA_Q_LORA, MLA_HEADS, MLA_QK)
    wq_rope = wq[..., MLA_NOPE:]
    wuq = jnp.concatenate([wq[..., :MLA_NOPE], wq_rope, swap_halves(wq_rope)], axis=-1)
    wuq = wuq.reshape(MLA_Q_LORA, MLA_HEADS * MLA_QK_PAD).astype(BF16)
    wkv = w_ukv.reshape(MLA_KV_LORA, MLA_HEADS, MLA_NOPE + MLA_V)
    wuk = wkv[..., :MLA_NOPE].reshape(MLA_KV_LORA, MLA_HEADS * MLA_NOPE).astype(BF16)
    wuv = wkv[..., MLA_NOPE:].reshape(MLA_KV_LORA, MLA_HEADS * MLA_V).astype(BF16)
    inv_freq = 1.0 / (ROPE_THETA ** (jnp.arange(0, MLA_ROPE, 2, dtype=F32) / MLA_ROPE))
    freq = jnp.concatenate([inv_freq, inv_freq, jnp.zeros((LANES - MLA_ROPE,), F32)]).reshape(1, LANES)
    pos = positions.astype(F32).reshape(t, 1)

    def rows(n):
        return pl.BlockSpec((tm, n), lambda i: (i, 0))

    qk_w = MLA_HEADS * MLA_QK_PAD
    v_w = MLA_HEADS * MLA_V
    vp_w = MLA_HEADS * MLA_V_PAD
    qf, kf, vf = pl.pallas_call(
        _mla_qkv_kernel,
        grid=(t // tm,),
        in_specs=[rows(MLA_Q_LORA), rows(MLA_KV_LORA), rows(LANES), rows(1),
                  _const_spec((1, MLA_Q_LORA)), _const_spec((1, MLA_KV_LORA)),
                  _const_spec(wuq.shape), _const_spec(wuk.shape), _const_spec(wuv.shape),
                  _const_spec((1, LANES))],
        out_specs=[rows(qk_w), rows(qk_w), rows(vp_w)],
        out_shape=[jax.ShapeDtypeStruct((t, qk_w), BF16), jax.ShapeDtypeStruct((t, qk_w), BF16),
                   jax.ShapeDtypeStruct((t, vp_w), BF16)],
        compiler_params=_params("parallel"),
        name="mla_qkv",
    )(cq, ckv, kr, pos, q_norm.reshape(1, -1), kv_norm.reshape(1, -1), wuq, wuk, wuv, freq)

    hps = FLASH_HEADS_PER_STEP
    o = pl.pallas_call(
        functools.partial(_flash_kernel, tq=tq, hps=hps),
        grid=(bsz, MLA_HEADS // hps, s // tq),
        in_specs=[pl.BlockSpec((1, tq, hps * MLA_QK_PAD), lambda b, hd, i: (b, i, hd)),
                  pl.BlockSpec((1, s, hps * MLA_QK_PAD), lambda b, hd, i: (b, 0, hd)),
                  pl.BlockSpec((1, s, hps * MLA_V_PAD), lambda b, hd, i: (b, 0, hd))],
        out_specs=pl.BlockSpec((1, tq, hps * MLA_V), lambda b, hd, i: (b, i, hd)),
        out_shape=jax.ShapeDtypeStruct((bsz, s, v_w), BF16),
        scratch_shapes=[pltpu.VMEM((hps, tq, MLA_V_PAD), F32)],
        compiler_params=_params("parallel", "parallel", "arbitrary"),
        name="mla_flash",
    )(qf.reshape(bsz, s, qk_w), kf.reshape(bsz, s, qk_w), vf.reshape(bsz, s, vp_w))
    return _out_proj(o.reshape(t, v_w), w_out.astype(BF16), h)


def kernel(x, positions, l0_ffn1_norm, l0_ffn1_w_gu, l0_ffn1_w_down, l0_mix_norm, l0_gla_w_in, l0_gla_w_gate_b, l0_gla_b_gate, l0_gla_norm, l0_gla_w_out, l0_ffn2_norm, l0_ffn2_w_gu, l0_ffn2_w_down, l1_ffn1_norm, l1_ffn1_w_gu, l1_ffn1_w_down, l1_mix_norm, l1_ssd_w_in, l1_ssd_conv_w, l1_ssd_conv_b, l1_ssd_dt_bias, l1_ssd_a_log, l1_ssd_d_skip, l1_ssd_norm, l1_ssd_w_out, l1_ffn2_norm, l1_ffn2_w_gu, l1_ffn2_w_down, l2_ffn1_norm, l2_ffn1_w_gu, l2_ffn1_w_down, l2_mix_norm, l2_mla_w_in, l2_mla_q_norm, l2_mla_w_uq, l2_mla_kv_norm, l2_mla_w_ukv, l2_mla_w_out, l2_ffn2_norm, l2_ffn2_w_gu, l2_ffn2_w_down, l3_ffn1_norm, l3_ffn1_w_gu, l3_ffn1_w_down, l3_mix_norm, l3_gla_w_in, l3_gla_w_gate_b, l3_gla_b_gate, l3_gla_norm, l3_gla_w_out, l3_ffn2_norm, l3_ffn2_w_gu, l3_ffn2_w_down, final_norm):
    bsz, s, d = x.shape
    h = x.reshape(bsz * s, d)

    h = _ffn(h, l0_ffn1_norm, l0_ffn1_w_gu, l0_ffn1_w_down)
    h = _gla_mixer(h, bsz, l0_mix_norm, l0_gla_w_in, l0_gla_w_gate_b, l0_gla_b_gate, l0_gla_norm, l0_gla_w_out)
    h = _ffn(h, l0_ffn2_norm, l0_ffn2_w_gu, l0_ffn2_w_down)

    h = _ffn(h, l1_ffn1_norm, l1_ffn1_w_gu, l1_ffn1_w_down)
    h = _ssd_mixer(h, bsz, l1_mix_norm, l1_ssd_w_in, l1_ssd_conv_w, l1_ssd_conv_b, l1_ssd_dt_bias,
                   l1_ssd_a_log, l1_ssd_d_skip, l1_ssd_norm, l1_ssd_w_out)
    h = _ffn(h, l1_ffn2_norm, l1_ffn2_w_gu, l1_ffn2_w_down)

    h = _ffn(h, l2_ffn1_norm, l2_ffn1_w_gu, l2_ffn1_w_down)
    h = _mla_mixer(h, bsz, positions, l2_mix_norm, l2_mla_w_in, l2_mla_q_norm, l2_mla_w_uq,
                   l2_mla_kv_norm, l2_mla_w_ukv, l2_mla_w_out)
    h = _ffn(h, l2_ffn2_norm, l2_ffn2_w_gu, l2_ffn2_w_down)

    h = _ffn(h, l3_ffn1_norm, l3_ffn1_w_gu, l3_ffn1_w_down)
    h = _gla_mixer(h, bsz, l3_mix_norm, l3_gla_w_in, l3_gla_w_gate_b, l3_gla_b_gate, l3_gla_norm, l3_gla_w_out)
    h = _ffn(h, l3_ffn2_norm, l3_ffn2_w_gu, l3_ffn2_w_down, final_w=final_norm)
    return h.reshape(bsz, s, d)
```

```python
import functools
import math

import jax
import jax.numpy as jnp
from jax import lax
from jax.experimental import pallas as pl
from jax.experimental.pallas import tpu as pltpu

F32 = jnp.float32
BF16 = jnp.bfloat16
EPS = 1e-6

V7X_VMEM_BYTES = 64 * 1024 * 1024
VMEM_LIMIT = V7X_VMEM_BYTES - 8 * 1024 * 1024
LANES = 128

D_FF = 2816
FF_CHUNK = 256

GLA_HEADS = 4
GLA_DK_H = 128
GLA_DV_H = 256
GLA_GATE_RANK = 16
GLA_GATE_TEMP = 16.0
GLA_CHUNK = 64

SSD_D_INNER = 2048
SSD_HEAD_DIM = 64
SSD_HEADS = 32
SSD_GROUPS = 4
SSD_HPG = 8
SSD_STATE = 128
SSD_CONV = 4
SSD_CHUNK = 128
SSD_CONV_DIM = 3072
SSD_HALO = 8

MLA_HEADS = 8
MLA_Q_LORA = 384
MLA_KV_LORA = 256
MLA_NOPE = 128
MLA_ROPE = 64
MLA_V = 128
MLA_QK = MLA_NOPE + MLA_ROPE
MLA_QK_PAD = 256
ROPE_THETA = 10000.0
MLA_V_PAD = 256
FLASH_HEADS_PER_STEP = 4

NT_DIMS = (((1,), (1,)), ((), ()))
TN_DIMS = (((0,), (0,)), ((), ()))


def _params(*sem):
    return pltpu.CompilerParams(dimension_semantics=sem, vmem_limit_bytes=VMEM_LIMIT)


def _const_spec(shape):
    return pl.BlockSpec(shape, lambda *_: (0,) * len(shape), pipeline_mode=pl.Buffered(1))


def _rms(x, w):
    return x * lax.rsqrt(jnp.mean(x * x, axis=-1, keepdims=True) + EPS) * w


def _silu(x):
    return x / (1.0 + jnp.exp(-x))


def _softplus(x):
    return jnp.maximum(x, 0.0) + jnp.log(1.0 + jnp.exp(-jnp.abs(x)))


def _dot(a, b):
    return jnp.dot(a, b, preferred_element_type=F32)


def _split_bf16(a, terms):
    parts = []
    for _ in range(terms):
        p = a.astype(BF16)
        parts.append(p)
        a = a - p.astype(F32)
    return parts


def _dot_exact_rhs(a, b, terms, dims=None):
    out = None
    for p in _split_bf16(a, terms):
        if dims is None:
            t = _dot(p, b)
        else:
            t = lax.dot_general(p, b, dims, preferred_element_type=F32)
        out = t if out is None else out + t
    return out


def _transpose_via_mxu(eye, a, terms=3):
    out = None
    for p in _split_bf16(a, terms):
        t = lax.dot_general(eye, p, NT_DIMS, preferred_element_type=F32)
        out = t if out is None else out + t
    return out


def _ffn_kernel(*refs, n_chunks, tf, final):
    if final:
        x_ref, nw_ref, wgu_ref, wd_ref, fw_ref, o_ref, xn_ref, acc_ref = refs
    else:
        x_ref, nw_ref, wgu_ref, wd_ref, o_ref, xn_ref, acc_ref = refs
    d_ff = n_chunks * tf
    xn_ref[...] = _rms(x_ref[...], nw_ref[...]).astype(BF16)

    def down(c):
        if isinstance(c, int):
            g_cols, u_cols = pl.ds(c * tf, tf), pl.ds(d_ff + c * tf, tf)
        else:
            g_cols = pl.ds(pl.multiple_of(c * tf, tf), tf)
            u_cols = pl.ds(pl.multiple_of(d_ff + c * tf, tf), tf)
        xn = xn_ref[...]
        a = (_silu(_dot(xn, wgu_ref[:, g_cols])) * _dot(xn, wgu_ref[:, u_cols])).astype(BF16)
        return _dot(a, wd_ref[g_cols, :])

    acc_ref[...] = down(0)

    for c in range(1, n_chunks - 1):
        acc_ref[...] += down(c)
    y = x_ref[...] + (acc_ref[...] + down(n_chunks - 1))
    if final:
        y = _rms(y, fw_ref[...])
    o_ref[...] = y


def _ffn(h, norm_w, w_gu, w_down, final_w=None, tm=1024):
    t, d = h.shape
    n_chunks = D_FF // FF_CHUNK
    wgu = w_gu.astype(BF16)
    wd = (0.5 * w_down).astype(BF16)
    args = [h, norm_w.reshape(1, d), wgu, wd]
    in_specs = [pl.BlockSpec((tm, d), lambda i: (i, 0)), _const_spec((1, d)),
                _const_spec(wgu.shape), _const_spec(wd.shape)]
    if final_w is not None:
        args.append(final_w.reshape(1, d))
        in_specs.append(_const_spec((1, d)))
    return pl.pallas_call(
        functools.partial(_ffn_kernel, n_chunks=n_chunks, tf=FF_CHUNK, final=final_w is not None),
        grid=(t // tm,),
        in_specs=in_specs,
        out_specs=pl.BlockSpec((tm, d), lambda i: (i, 0)),
        out_shape=jax.ShapeDtypeStruct((t, d), F32),
        scratch_shapes=[pltpu.VMEM((tm, d), BF16), pltpu.VMEM((tm, d), F32)],
        compiler_params=_params("parallel"),
        name="ffn",
    )(*args)


def _out_kernel(a_ref, w_ref, h_ref, o_ref):
    o_ref[...] = h_ref[...] + _dot(a_ref[...], w_ref[...])


def _out_proj(a, w, h, tm=1024):
    t, d = h.shape
    k = a.shape[1]
    return pl.pallas_call(
        _out_kernel,
        grid=(t // tm,),
        in_specs=[pl.BlockSpec((tm, k), lambda i: (i, 0)), _const_spec(w.shape),
                  pl.BlockSpec((tm, d), lambda i: (i, 0))],
        out_specs=pl.BlockSpec((tm, d), lambda i: (i, 0)),
        out_shape=jax.ShapeDtypeStruct((t, d), F32),
        compiler_params=_params("parallel"),
        name="out_proj",
    )(a, w, h)


def _pad_cols(w, n):
    return jnp.pad(w, ((0, 0), (0, n - w.shape[1])))


def _gla_kernel(h_ref, hn_ref, nw_ref, win_ref, wgb_ref, bg_ref, gnw_ref, wout_ref, o_ref,
                st_ref, xn_ref, og_buf, q_buf, k_buf, v_buf, g_buf, la_buf, *, tb):
    c_len, heads, dk, dv = GLA_CHUNK, GLA_HEADS, GLA_DK_H, GLA_DV_H
    dk_all, dv_all = heads * dk, heads * dv
    q0, k0, v0, g0, a0 = 0, dk_all, 2 * dk_all, 2 * dk_all + dv_all, 2 * dk_all + 2 * dv_all

    row = lax.broadcasted_iota(jnp.int32, (c_len, c_len), 0)
    col = lax.broadcasted_iota(jnp.int32, (c_len, c_len), 1)
    causal = col <= row
    tril = jnp.where(causal, 1.0, 0.0).astype(BF16)
    scale = dk ** -0.5

    def projection(load_rows, slot):
        def norm():
            xn_ref[...] = _rms(load_rows(), nw_ref[...]).astype(BF16)

        def proj_q():
            q_buf[slot] = _dot(xn_ref[...], win_ref[:, q0:k0])

        def proj_k():
            k_buf[slot] = _dot(xn_ref[...], win_ref[:, k0:v0])

        def proj_v():
            v_buf[slot] = _dot(xn_ref[...], win_ref[:, v0:g0]).astype(BF16)

        def proj_g():
            g_buf[slot] = _dot(xn_ref[...], win_ref[:, g0:a0])

        def gate():
            a_hi, a_lo = _split_bf16(_dot(xn_ref[...], win_ref[:, a0:a0 + LANES]), 2)
            z = _dot(a_hi, wgb_ref[0]) + (_dot(a_hi, wgb_ref[1]) + _dot(a_lo, wgb_ref[0])) + bg_ref[...]
            la_buf[slot] = (jnp.minimum(z, 0.0) - jnp.log(1.0 + jnp.exp(-jnp.abs(z)))) / GLA_GATE_TEMP

        return [norm, proj_q, proj_k, proj_v, proj_g, gate]

    def chunk(c, slot):
        rows = pl.ds(c * c_len, c_len)
        b = None
        for p in _split_bf16(la_buf[slot, rows, :], 3):
            t = _dot(tril, p)
            b = t if b is None else b + t
        b_last = b[c_len - 1:c_len, :]
        q = q_buf[slot, rows, :]
        k = k_buf[slot, rows, :]
        q_dec = (q * scale * jnp.exp(b)).astype(BF16)
        k_inv = (k * jnp.exp(-b)).astype(BF16)
        k_end = (k * jnp.exp(b_last - b)).astype(BF16)
        decay = jnp.exp(b_last)
        v = v_buf[slot, rows, :]
        outs = []
        for h in range(heads):
            ks = slice(h * dk, (h + 1) * dk)
            vh = v[:, h * dv:(h + 1) * dv]
            qd = q_dec[:, ks]
            attn = lax.dot_general(qd, k_inv[:, ks], NT_DIMS, preferred_element_type=F32)
            attn = jnp.where(causal, attn, 0.0).astype(BF16)
            st = st_ref[h]
            o = _dot(attn, vh) + lax.dot_general(qd, st.astype(BF16), NT_DIMS,
                                                 preferred_element_type=F32)
            st_ref[h] = st * decay[:, ks] + lax.dot_general(vh, k_end[:, ks], TN_DIMS,
                                                            preferred_element_type=F32)
            outs.append(o * lax.rsqrt(jnp.mean(o * o, axis=-1, keepdims=True) + EPS))
        o = jnp.concatenate(outs, axis=-1)
        o = o * gnw_ref[...] * _silu(g_buf[slot, rows, :])
        og_buf[slot, rows, :] = o.astype(BF16)

    def run_block(slot, out_rows, fill_next):
        n_chunks = tb // c_len
        for c in range(n_chunks):
            chunk(c, slot)
            if c < len(fill_next):
                fill_next[c]()
        for step in fill_next[n_chunks:]:
            step()
        o_ref[0, out_rows, :] = h_ref[0, out_rows, :] + _dot(og_buf[slot], wout_ref[...])

    @pl.when(pl.program_id(1) == 0)
    def _():
        st_ref[...] = jnp.zeros_like(st_ref)
        for step in projection(lambda: h_ref[0, 0:tb, :], 0):
            step()

    run_block(0, pl.ds(0, tb), projection(lambda: h_ref[0, tb:2 * tb, :], 1))
    run_block(1, pl.ds(tb, tb), projection(lambda: hn_ref[0], 0))


def _gla_mixer(h, bsz, mix_norm, w_in, w_gate_b, b_gate, gn_w, w_out, tb=512):
    t, d = h.shape
    s = t // bsz
    dk_all, dv_all = GLA_HEADS * GLA_DK_H, GLA_HEADS * GLA_DV_H
    win = _pad_cols(w_in, 2 * dk_all + 2 * dv_all + LANES).astype(BF16)
    wgb = jnp.pad(w_gate_b, ((0, LANES - GLA_GATE_RANK), (0, 0)))
    wgb_hi = wgb.astype(BF16)
    wgb = jnp.stack([wgb_hi, (wgb - wgb_hi.astype(F32)).astype(BF16)])
    wout = w_out.astype(BF16)
    last_blk = s // tb - 1
    h3 = h.reshape(bsz, s, d)
    out = pl.pallas_call(
        functools.partial(_gla_kernel, tb=tb),
        grid=(bsz, s // (2 * tb)),
        in_specs=[pl.BlockSpec((1, 2 * tb, d), lambda b, i: (b, i, 0)),
                  pl.BlockSpec((1, tb, d), lambda b, i: (b, jnp.minimum(2 * i + 2, last_blk), 0)),
                  _const_spec((1, d)), _const_spec(win.shape), _const_spec(wgb.shape),
                  _const_spec((1, dk_all)), _const_spec((1, dv_all)), _const_spec(wout.shape)],
        out_specs=pl.BlockSpec((1, 2 * tb, d), lambda b, i: (b, i, 0)),
        out_shape=jax.ShapeDtypeStruct((bsz, s, d), F32),
        scratch_shapes=[pltpu.VMEM((GLA_HEADS, GLA_DV_H, GLA_DK_H), F32),
                        pltpu.VMEM((tb, d), BF16),
                        pltpu.VMEM((2, tb, dv_all), BF16),
                        pltpu.VMEM((2, tb, dk_all), F32), pltpu.VMEM((2, tb, dk_all), F32),
                        pltpu.VMEM((2, tb, dv_all), BF16), pltpu.VMEM((2, tb, dv_all), F32),
                        pltpu.VMEM((2, tb, dk_all), F32)],
        compiler_params=_params("parallel", "arbitrary"),
        name="gla",
    )(h3, h3, mix_norm.reshape(1, d), win, wgb, b_gate.reshape(1, dk_all), gn_w.reshape(1, dv_all), wout)
    return out.reshape(t, d)


def _ssd_kernel(h_ref, hn_ref, mnw_ref, win_ref, cw_ref, cb_ref, dtb_ref, alog_ref, dsk_ref, nw_ref, e_ref,
                wout_ref, o_ref, st_ref, xn_ref, tail_ref, conv_ref, z_buf, xbc_buf, dt_buf, y_buf, *, tb):
    q_len, n_grp, hpg, p_dim, n_st = SSD_CHUNK, SSD_GROUPS, SSD_HPG, SSD_HEAD_DIM, SSD_STATE
    gw = hpg * p_dim
    z0, x0, d0 = 0, SSD_D_INNER, SSD_D_INNER + SSD_CONV_DIM
    conv_piece = SSD_CONV_DIM // 3

    def projection(load_rows, slot):
        def norm():
            xn_ref[...] = _rms(load_rows(), mnw_ref[...]).astype(BF16)

        def proj_z(lo, hi):
            z_buf[slot, :, lo:hi] = _dot(xn_ref[...], win_ref[:, z0 + lo:z0 + hi])

        def proj_x(lo, hi):
            xbc_buf[slot, :, lo:hi] = _dot(xn_ref[...], win_ref[:, x0 + lo:x0 + hi])

        def proj_dt():
            dt_buf[slot] = _dot(xn_ref[...], win_ref[:, d0:d0 + LANES])

        half = SSD_D_INNER // 2
        steps = [norm, functools.partial(proj_z, 0, half), functools.partial(proj_z, half, SSD_D_INNER)]
        steps += [functools.partial(proj_x, j * conv_piece, (j + 1) * conv_piece) for j in range(3)]
        return steps + [proj_dt]

    def conv_step(slot, lo, hi):
        u = jnp.concatenate([tail_ref[:, lo:hi], xbc_buf[slot, :, lo:hi]], axis=0)
        tiles = u.reshape((tb + SSD_HALO) // SSD_HALO, SSD_HALO, hi - lo)
        sub = lax.broadcasted_iota(jnp.int32, (tb // SSD_HALO, SSD_HALO, hi - lo), 1)
        acc = cb_ref[:, lo:hi] + cw_ref[SSD_CONV - 1:SSD_CONV, lo:hi] * tiles[1:]
        for delay in range(1, SSD_CONV):
            rot = pltpu.roll(tiles, delay, 1)
            acc = acc + cw_ref[SSD_CONV - 1 - delay:SSD_CONV - delay, lo:hi] * jnp.where(
                sub >= delay, rot[1:], rot[:-1])
        conv_ref[:, lo:hi] = _silu(acc).reshape(tb, hi - lo)
        tail_ref[:, lo:hi] = xbc_buf[slot, tb - SSD_HALO:tb, lo:hi]

    row = lax.broadcasted_iota(jnp.int32, (q_len, q_len), 0)
    col = lax.broadcasted_iota(jnp.int32, (q_len, q_len), 1)
    causal = col <= row
    tril = jnp.where(causal, 1.0, 0.0).astype(BF16)
    er = lax.broadcasted_iota(jnp.int32, (LANES, LANES), 0)
    ec = lax.broadcasted_iota(jnp.int32, (LANES, LANES), 1)
    eye = jnp.where(er == ec, 1.0, 0.0).astype(BF16)
    lane = lax.broadcasted_iota(jnp.int32, (q_len, LANES), 1)
    low_half = lane < p_dim
    a_neg = -jnp.exp(alog_ref[...])

    def chunk_steps(c, slot):
        rows = pl.ds(c * q_len, q_len)
        shared = {}

        def decays():
            dt = _softplus(dt_buf[slot, rows, :] + dtb_ref[...])
            da = dt * a_neg
            a_cs = None
            for p in _split_bf16(da, 3):
                t = _dot(tril, p)
                a_cs = t if a_cs is None else a_cs + t
            a_last = a_cs[q_len - 1:q_len, :]
            a_cs_t = _transpose_via_mxu(eye, a_cs)
            shared["shift_t"] = a_cs_t - jnp.log(_transpose_via_mxu(eye, dt))
            e_mat = e_ref[...]
            shared["a_cs"] = a_cs
            shared["ea_x"] = _dot_exact_rhs(jnp.exp(a_cs), e_mat, 2)
            shared["w_x"] = _dot_exact_rhs(jnp.exp(a_last - a_cs) * dt, e_mat, 2)

        def group(g):
            a_cs, shift_t, ea_x, w_x = shared["a_cs"], shared["shift_t"], shared["ea_x"], shared["w_x"]
            tot_x = ea_x[q_len - 1:q_len, :]
            gs = slice(g * gw, (g + 1) * gw)
            x_g = conv_ref[rows, gs]
            b_g = conv_ref[rows, SSD_D_INNER + g * n_st:SSD_D_INNER + (g + 1) * n_st].astype(BF16)
            c_g = conv_ref[rows, SSD_D_INNER + (n_grp + g) * n_st:
                           SSD_D_INNER + (n_grp + g + 1) * n_st].astype(BF16)
            cb = lax.dot_general(c_g, b_g, NT_DIMS, preferred_element_type=F32)
            st = st_ref[g]
            y_off = _dot(c_g, st.astype(BF16)) * ea_x[:, gs]
            xw = (x_g * w_x[:, gs]).astype(BF16)
            st_ref[g] = st * tot_x[:, gs] + lax.dot_general(b_g, xw, TN_DIMS, preferred_element_type=F32)
            xb = x_g.astype(BF16)
            ys = []
            for pr in range(hpg // 2):
                h1 = g * hpg + 2 * pr
                x_pair = xb[:, pr * LANES:(pr + 1) * LANES]
                y_pair = None
                for hh, keep in ((h1, low_half), (h1 + 1, jnp.logical_not(low_half))):
                    seg = a_cs[:, hh:hh + 1] - shift_t[hh:hh + 1, :]
                    m = cb * jnp.where(causal, jnp.exp(seg), 0.0)
                    t = _dot(m.astype(BF16), jnp.where(keep, x_pair, jnp.zeros_like(x_pair)))
                    y_pair = t if y_pair is None else y_pair + t
                ys.append(y_pair)
            y = jnp.concatenate(ys, axis=-1) + y_off + x_g * dsk_ref[:, gs]
            y = y * _silu(z_buf[slot, rows, gs])
            y = y * lax.rsqrt(jnp.mean(y * y, axis=-1, keepdims=True) + EPS) * nw_ref[:, gs]
            y_buf[slot, rows, gs] = y.astype(BF16)

        return [decays] + [functools.partial(group, g) for g in range(n_grp)]

    def run_block(slot, out_rows, fill_next):
        work = [functools.partial(conv_step, slot, j * conv_piece, (j + 1) * conv_piece) for j in range(3)]
        for c in range(tb // q_len):
            work += chunk_steps(c, slot)
        for j, step in enumerate(work):
            step()
            if j < len(fill_next):
                fill_next[j]()
        for step in fill_next[len(work):]:
            step()
        o_ref[0, out_rows, :] = h_ref[0, out_rows, :] + _dot(y_buf[slot], wout_ref[...])

    @pl.when(pl.program_id(1) == 0)
    def _():
        st_ref[...] = jnp.zeros_like(st_ref)
        tail_ref[...] = jnp.zeros_like(tail_ref)
        for step in projection(lambda: h_ref[0, 0:tb, :], 0):
            step()

    run_block(0, pl.ds(0, tb), projection(lambda: h_ref[0, tb:2 * tb, :], 1))
    run_block(1, pl.ds(tb, tb), projection(lambda: hn_ref[0], 0))


def _ssd_mixer(h, bsz, mix_norm, w_in, conv_w, conv_b, dt_bias, a_log, d_skip, norm_w, w_out, tb=256):
    t, d = h.shape
    s = t // bsz
    win = _pad_cols(w_in, SSD_D_INNER + SSD_CONV_DIM + LANES).astype(BF16)
    wout = w_out.astype(BF16)
    head_of_lane = jnp.arange(SSD_D_INNER) // SSD_HEAD_DIM
    expand = (jnp.arange(LANES)[:, None] == head_of_lane[None, :]).astype(BF16)

    def pad_row(v):
        return jnp.pad(v, (0, LANES - v.shape[0])).reshape(1, LANES)

    last_blk = s // tb - 1
    h3 = h.reshape(bsz, s, d)
    out = pl.pallas_call(
        functools.partial(_ssd_kernel, tb=tb),
        grid=(bsz, s // (2 * tb)),
        in_specs=[pl.BlockSpec((1, 2 * tb, d), lambda b, i: (b, i, 0)),
                  pl.BlockSpec((1, tb, d), lambda b, i: (b, jnp.minimum(2 * i + 2, last_blk), 0)),
                  _const_spec((1, d)), _const_spec(win.shape),
                  _const_spec((SSD_CONV, SSD_CONV_DIM)), _const_spec((1, SSD_CONV_DIM)),
                  _const_spec((1, LANES)), _const_spec((1, LANES)),
                  _const_spec((1, SSD_D_INNER)), _const_spec((1, SSD_D_INNER)),
                  _const_spec((LANES, SSD_D_INNER)), _const_spec(wout.shape)],
        out_specs=pl.BlockSpec((1, 2 * tb, d), lambda b, i: (b, i, 0)),
        out_shape=jax.ShapeDtypeStruct((bsz, s, d), F32),
        scratch_shapes=[pltpu.VMEM((SSD_GROUPS, SSD_STATE, SSD_HPG * SSD_HEAD_DIM), F32),
                        pltpu.VMEM((tb, d), BF16),
                        pltpu.VMEM((SSD_HALO, SSD_CONV_DIM), F32),
                        pltpu.VMEM((tb, SSD_CONV_DIM), F32),
                        pltpu.VMEM((2, tb, SSD_D_INNER), F32),
                        pltpu.VMEM((2, tb, SSD_CONV_DIM), F32),
                        pltpu.VMEM((2, tb, LANES), F32),
                        pltpu.VMEM((2, tb, SSD_D_INNER), BF16)],
        compiler_params=_params("parallel", "arbitrary"),
        name="ssd",
    )(h3, h3, mix_norm.reshape(1, d), win, conv_w, conv_b.reshape(1, -1), pad_row(dt_bias), pad_row(a_log),
      jnp.repeat(d_skip, SSD_HEAD_DIM).reshape(1, -1), norm_w.reshape(1, -1), expand, wout)
    return out.reshape(t, d)


def _mla_qkv_kernel(h_ref, pos_ref, nw_ref, win_ref, qn_ref, kvn_ref, wuq_ref, wuk_ref, wuv_ref, freq_ref,
                    q_ref, k_ref, v_ref):
    half = MLA_ROPE // 2
    o1, o2 = MLA_Q_LORA, MLA_Q_LORA + MLA_KV_LORA
    lat = _dot(_rms(h_ref[...], nw_ref[...]).astype(BF16), win_ref[...])
    cqn = _rms(lat[:, :o1], qn_ref[...]).astype(BF16)
    ckvn = _rms(lat[:, o1:o2], kvn_ref[...]).astype(BF16)
    ang = pos_ref[...] * freq_ref[...]
    lane = lax.broadcasted_iota(jnp.int32, ang.shape, 1)
    cos_v = jnp.where(lane < MLA_ROPE, jnp.cos(ang), 0.0)
    sin_a = jnp.sin(ang)
    sin_v = jnp.where(lane < half, -sin_a, jnp.where(lane < MLA_ROPE, sin_a, 0.0))

    def rope(t):
        return t * cos_v + pltpu.roll(t, MLA_ROPE, 1) * sin_v

    scale = MLA_QK ** -0.5 * math.log2(math.e)
    q = _dot(cqn, wuq_ref[...])
    k_rope = rope(lat[:, o2:o2 + LANES]).astype(BF16)
    k_nope = _dot(ckvn, wuk_ref[...])
    for h in range(MLA_HEADS):
        base = h * MLA_QK_PAD
        q_ref[:, base:base + MLA_NOPE] = (q[:, base:base + MLA_NOPE] * scale).astype(BF16)
        q_ref[:, base + MLA_NOPE:base + MLA_QK_PAD] = (
            rope(q[:, base + MLA_NOPE:base + MLA_QK_PAD]) * scale).astype(BF16)
        k_ref[:, base:base + MLA_NOPE] = k_nope[:, h * MLA_NOPE:(h + 1) * MLA_NOPE].astype(BF16)
        k_ref[:, base + MLA_NOPE:base + MLA_QK_PAD] = k_rope
    v = _dot(ckvn, wuv_ref[...])
    ones_col = jnp.where(lane == 0, 1.0, 0.0).astype(BF16)
    for h in range(MLA_HEADS):
        base = h * MLA_V_PAD
        v_ref[:, base:base + MLA_V] = v[:, h * MLA_V:(h + 1) * MLA_V].astype(BF16)
        v_ref[:, base + MLA_V:base + MLA_V_PAD] = ones_col


def _flash_kernel(q_ref, k_ref, v_ref, o_ref, acc_ref, *, tq, hps):
    qi = pl.program_id(2)
    acc_ref[...] = jnp.zeros_like(acc_ref)
    row = lax.broadcasted_iota(jnp.int32, (tq, tq), 0)
    col = lax.broadcasted_iota(jnp.int32, (tq, tq), 1)

    def step(j, ms, masked):
        ks = pl.ds(pl.multiple_of(j * tq, tq), tq)
        new_ms = []
        for hd in range(hps):
            q = q_ref[0, :, hd * MLA_QK_PAD:(hd + 1) * MLA_QK_PAD]
            k = k_ref[0, ks, hd * MLA_QK_PAD:(hd + 1) * MLA_QK_PAD]
            s = lax.dot_general(q, k, NT_DIMS, preferred_element_type=F32)
            if masked:
                s = jnp.where(col <= row, s, -jnp.inf)
            m_new = jnp.maximum(ms[hd], jnp.max(s, axis=-1, keepdims=True))
            p = jnp.exp2((s - m_new).astype(BF16))
            v = v_ref[0, ks, hd * MLA_V_PAD:(hd + 1) * MLA_V_PAD]
            acc_ref[hd] = jnp.exp2(ms[hd] - m_new) * acc_ref[hd] + _dot(p, v)
            new_ms.append(m_new)
        return tuple(new_ms)

    init = tuple(jnp.full((tq, 1), -jnp.inf, F32) for _ in range(hps))
    ms = lax.fori_loop(0, qi, lambda j, cr: step(j, cr, False), init)
    step(qi, ms, True)
    for hd in range(hps):
        acc = acc_ref[hd]
        o_ref[0, :, hd * MLA_V:(hd + 1) * MLA_V] = (acc[:, :MLA_V] / acc[:, MLA_V:MLA_V + 1]).astype(o_ref.dtype)


def _mla_mixer(h, bsz, positions, mix_norm, w_in, q_norm, w_uq, kv_norm, w_ukv, w_out, tm=512, tq=512):
    t, d = h.shape
    s = t // bsz
    half = MLA_ROPE // 2

    def swap_halves(w):
        return jnp.concatenate([w[..., half:], w[..., :half]], axis=-1)

    o1, o2 = MLA_Q_LORA, MLA_Q_LORA + MLA_KV_LORA
    win = jnp.concatenate([w_in, swap_halves(w_in[:, o2:])], axis=-1).astype(BF16)

    wq = w_uq.reshape(MLA_Q_LORA, MLA_HEADS, MLA_QK)
    wq_rope = wq[..., MLA_NOPE:]
    wuq = jnp.concatenate([wq[..., :MLA_NOPE], wq_rope, swap_halves(wq_rope)], axis=-1)
    wuq = wuq.reshape(MLA_Q_LORA, MLA_HEADS * MLA_QK_PAD).astype(BF16)
    wkv = w_ukv.reshape(MLA_KV_LORA, MLA_HEADS, MLA_NOPE + MLA_V)
    wuk = wkv[..., :MLA_NOPE].reshape(MLA_KV_LORA, MLA_HEADS * MLA_NOPE).astype(BF16)
    wuv = wkv[..., MLA_NOPE:].reshape(MLA_KV_LORA, MLA_HEADS * MLA_V).astype(BF16)
    inv_freq = 1.0 / (ROPE_THETA ** (jnp.arange(0, MLA_ROPE, 2, dtype=F32) / MLA_ROPE))
    freq = jnp.concatenate([inv_freq, inv_freq, jnp.zeros((LANES - MLA_ROPE,), F32)]).reshape(1, LANES)
    pos = positions.astype(F32).reshape(t, 1)

    def rows(n):
        return pl.BlockSpec((tm, n), lambda i: (i, 0))

    qk_w = MLA_HEADS * MLA_QK_PAD
    v_w = MLA_HEADS * MLA_V
    vp_w = MLA_HEADS * MLA_V_PAD
    qf, kf, vf = pl.pallas_call(
        _mla_qkv_kernel,
        grid=(t // tm,),
        in_specs=[rows(d), rows(1), _const_spec((1, d)), _const_spec(win.shape),
                  _const_spec((1, MLA_Q_LORA)), _const_spec((1, MLA_KV_LORA)),
                  _const_spec(wuq.shape), _const_spec(wuk.shape), _const_spec(wuv.shape),
                  _const_spec((1, LANES))],
        out_specs=[rows(qk_w), rows(qk_w), rows(vp_w)],
        out_shape=[jax.ShapeDtypeStruct((t, qk_w), BF16), jax.ShapeDtypeStruct((t, qk_w), BF16),
                   jax.ShapeDtypeStruct((t, vp_w), BF16)],
        compiler_params=_params("parallel"),
        name="mla_qkv",
    )(h, pos, mix_norm.reshape(1, d), win, q_norm.reshape(1, -1), kv_norm.reshape(1, -1), wuq, wuk, wuv, freq)

    hps = FLASH_HEADS_PER_STEP
    o = pl.pallas_call(
        functools.partial(_flash_kernel, tq=tq, hps=hps),
        grid=(bsz, MLA_HEADS // hps, s // tq),
        in_specs=[pl.BlockSpec((1, tq, hps * MLA_QK_PAD), lambda b, hd, i: (b, i, hd)),
                  pl.BlockSpec((1, s, hps * MLA_QK_PAD), lambda b, hd, i: (b, 0, hd)),
                  pl.BlockSpec((1, s, hps * MLA_V_PAD), lambda b, hd, i: (b, 0, hd))],
        out_specs=pl.BlockSpec((1, tq, hps * MLA_V), lambda b, hd, i: (b, i, hd)),
        out_shape=jax.ShapeDtypeStruct((bsz, s, v_w), BF16),
        scratch_shapes=[pltpu.VMEM((hps, tq, MLA_V_PAD), F32)],
        compiler_params=_params("parallel", "parallel", "arbitrary"),
        name="mla_flash",
    )(qf.reshape(bsz, s, qk_w), kf.reshape(bsz, s, qk_w), vf.reshape(bsz, s, vp_w))
    return _out_proj(o.reshape(t, v_w), w_out.astype(BF16), h)


def kernel(x, positions, l0_ffn1_norm, l0_ffn1_w_gu, l0_ffn1_w_down, l0_mix_norm, l0_gla_w_in, l0_gla_w_gate_b, l0_gla_b_gate, l0_gla_norm, l0_gla_w_out, l0_ffn2_norm, l0_ffn2_w_gu, l0_ffn2_w_down, l1_ffn1_norm, l1_ffn1_w_gu, l1_ffn1_w_down, l1_mix_norm, l1_ssd_w_in, l1_ssd_conv_w, l1_ssd_conv_b, l1_ssd_dt_bias, l1_ssd_a_log, l1_ssd_d_skip, l1_ssd_norm, l1_ssd_w_out, l1_ffn2_norm, l1_ffn2_w_gu, l1_ffn2_w_down, l2_ffn1_norm, l2_ffn1_w_gu, l2_ffn1_w_down, l2_mix_norm, l2_mla_w_in, l2_mla_q_norm, l2_mla_w_uq, l2_mla_kv_norm, l2_mla_w_ukv, l2_mla_w_out, l2_ffn2_norm, l2_ffn2_w_gu, l2_ffn2_w_down, l3_ffn1_norm, l3_ffn1_w_gu, l3_ffn1_w_down, l3_mix_norm, l3_gla_w_in, l3_gla_w_gate_b, l3_gla_b_gate, l3_gla_norm, l3_gla_w_out, l3_ffn2_norm, l3_ffn2_w_gu, l3_ffn2_w_down, final_norm):
    bsz, s, d = x.shape
    h = x.reshape(bsz * s, d)

    h = _ffn(h, l0_ffn1_norm, l0_ffn1_w_gu, l0_ffn1_w_down)
    h = _gla_mixer(h, bsz, l0_mix_norm, l0_gla_w_in, l0_gla_w_gate_b, l0_gla_b_gate, l0_gla_norm, l0_gla_w_out)
    h = _ffn(h, l0_ffn2_norm, l0_ffn2_w_gu, l0_ffn2_w_down)

    h = _ffn(h, l1_ffn1_norm, l1_ffn1_w_gu, l1_ffn1_w_down)
    h = _ssd_mixer(h, bsz, l1_mix_norm, l1_ssd_w_in, l1_ssd_conv_w, l1_ssd_conv_b, l1_ssd_dt_bias,
                   l1_ssd_a_log, l1_ssd_d_skip, l1_ssd_norm, l1_ssd_w_out)
    h = _ffn(h, l1_ffn2_norm, l1_ffn2_w_gu, l1_ffn2_w_down)

    h = _ffn(h, l2_ffn1_norm, l2_ffn1_w_gu, l2_ffn1_w_down)
    h = _mla_mixer(h, bsz, positions, l2_mix_norm, l2_mla_w_in, l2_mla_q_norm, l2_mla_w_uq,
                   l2_mla_kv_norm, l2_mla_w_ukv, l2_mla_w_out)
    h = _ffn(h, l2_ffn2_norm, l2_ffn2_w_gu, l2_ffn2_w_down)

    h = _ffn(h, l3_ffn1_norm, l3_ffn1_w_gu, l3_ffn1_w_down)
    h = _gla_mixer(h, bsz, l3_mix_norm, l3_gla_w_in, l3_gla_w_gate_b, l3_gla_b_gate, l3_gla_norm, l3_gla_w_out)
    h = _ffn(h, l3_ffn2_norm, l3_ffn2_w_gu, l3_ffn2_w_down, final_w=final_norm)
    return h.reshape(bsz, s, d)
```

```python
import functools
import math

import jax
import jax.numpy as jnp
from jax import lax
from jax.experimental import pallas as pl
from jax.experimental.pallas import tpu as pltpu

F32 = jnp.float32
BF16 = jnp.bfloat16
EPS = 1e-6

V7X_VMEM_BYTES = 64 * 1024 * 1024
VMEM_LIMIT = V7X_VMEM_BYTES - 8 * 1024 * 1024
LANES = 128

D_FF = 2816
FF_CHUNK = 256

GLA_HEADS = 4
GLA_DK_H = 128
GLA_DV_H = 256
GLA_GATE_RANK = 16
GLA_GATE_TEMP = 16.0
GLA_CHUNK = 64

SSD_D_INNER = 2048
SSD_HEAD_DIM = 64
SSD_HEADS = 32
SSD_GROUPS = 4
SSD_HPG = 8
SSD_STATE = 128
SSD_CONV = 4
SSD_CHUNK = 128
SSD_CONV_DIM = 3072
SSD_HALO = 8

MLA_HEADS = 8
MLA_Q_LORA = 384
MLA_KV_LORA = 256
MLA_NOPE = 128
MLA_ROPE = 64
MLA_V = 128
MLA_QK = MLA_NOPE + MLA_ROPE
MLA_QK_PAD = 256
ROPE_THETA = 10000.0
MLA_V_PAD = 256
FLASH_HEADS_PER_STEP = 4

NT_DIMS = (((1,), (1,)), ((), ()))
TN_DIMS = (((0,), (0,)), ((), ()))


def _params(*sem):
    return pltpu.CompilerParams(dimension_semantics=sem, vmem_limit_bytes=VMEM_LIMIT)


def _const_spec(shape):
    return pl.BlockSpec(shape, lambda *_: (0,) * len(shape), pipeline_mode=pl.Buffered(1))


def _rms(x, w):
    return x * lax.rsqrt(jnp.mean(x * x, axis=-1, keepdims=True) + EPS) * w


def _silu(x):
    return x / (1.0 + jnp.exp(-x))


def _softplus(x):
    return jnp.maximum(x, 0.0) + jnp.log(1.0 + jnp.exp(-jnp.abs(x)))


def _dot(a, b):
    return jnp.dot(a, b, preferred_element_type=F32)


def _split_bf16(a, terms):
    parts = []
    for _ in range(terms):
        p = a.astype(BF16)
        parts.append(p)
        a = a - p.astype(F32)
    return parts


def _dot_exact_rhs(a, b, terms, dims=None):
    out = None
    for p in _split_bf16(a, terms):
        if dims is None:
            t = _dot(p, b)
        else:
            t = lax.dot_general(p, b, dims, preferred_element_type=F32)
        out = t if out is None else out + t
    return out


def _transpose_via_mxu(eye, a, terms=3):
    out = None
    for p in _split_bf16(a, terms):
        t = lax.dot_general(eye, p, NT_DIMS, preferred_element_type=F32)
        out = t if out is None else out + t
    return out


def _ffn_kernel(*refs, n_chunks, tf, final, mixer_out):
    refs = list(refs)
    x_ref = refs.pop(0)
    if mixer_out:
        a_ref, wo_ref = refs.pop(0), refs.pop(0)
    nw_ref, wgu_ref, wd_ref = refs.pop(0), refs.pop(0), refs.pop(0)
    if final:
        fw_ref = refs.pop(0)
    o_ref, xn_ref, acc_ref = refs[:3]
    if mixer_out:
        x_in = refs[3]
        x_in[...] = x_ref[...] + _dot(a_ref[...], wo_ref[...])
    else:
        x_in = x_ref
    x_ref = x_in
    d_ff = n_chunks * tf
    xn_ref[...] = _rms(x_ref[...], nw_ref[...]).astype(BF16)

    def down(c):
        if isinstance(c, int):
            g_cols, u_cols = pl.ds(c * tf, tf), pl.ds(d_ff + c * tf, tf)
        else:
            g_cols = pl.ds(pl.multiple_of(c * tf, tf), tf)
            u_cols = pl.ds(pl.multiple_of(d_ff + c * tf, tf), tf)
        xn = xn_ref[...]
        a = (_silu(_dot(xn, wgu_ref[:, g_cols])) * _dot(xn, wgu_ref[:, u_cols])).astype(BF16)
        return _dot(a, wd_ref[g_cols, :])

    acc_ref[...] = down(0)

    for c in range(1, n_chunks - 1):
        acc_ref[...] += down(c)
    y = x_ref[...] + (acc_ref[...] + down(n_chunks - 1))
    if final:
        y = _rms(y, fw_ref[...])
    o_ref[...] = y


def _ffn(h, norm_w, w_gu, w_down, final_w=None, mixer_out=None, tm=1024):
    t, d = h.shape
    n_chunks = D_FF // FF_CHUNK
    wgu = w_gu.astype(BF16)
    wd = (0.5 * w_down).astype(BF16)
    args = [h]
    in_specs = [pl.BlockSpec((tm, d), lambda i: (i, 0))]
    scratch = [pltpu.VMEM((tm, d), BF16), pltpu.VMEM((tm, d), F32)]
    if mixer_out is not None:
        a, w_out = mixer_out
        args += [a, w_out]
        in_specs += [pl.BlockSpec((tm, a.shape[1]), lambda i: (i, 0)), _const_spec(w_out.shape)]
        scratch.append(pltpu.VMEM((tm, d), F32))
    args += [norm_w.reshape(1, d), wgu, wd]
    in_specs += [_const_spec((1, d)), _const_spec(wgu.shape), _const_spec(wd.shape)]
    if final_w is not None:
        args.append(final_w.reshape(1, d))
        in_specs.append(_const_spec((1, d)))
    return pl.pallas_call(
        functools.partial(_ffn_kernel, n_chunks=n_chunks, tf=FF_CHUNK, final=final_w is not None,
                          mixer_out=mixer_out is not None),
        grid=(t // tm,),
        in_specs=in_specs,
        out_specs=pl.BlockSpec((tm, d), lambda i: (i, 0)),
        out_shape=jax.ShapeDtypeStruct((t, d), F32),
        scratch_shapes=scratch,
        compiler_params=_params("parallel"),
        name="ffn",
    )(*args)


def _pad_cols(w, n):
    return jnp.pad(w, ((0, 0), (0, n - w.shape[1])))


def _gla_kernel(h_ref, hn_ref, nw_ref, win_ref, wgb_ref, bg_ref, gnw_ref, wout_ref, o_ref,
                st_ref, xn_ref, og_buf, q_buf, k_buf, v_buf, g_buf, la_buf, *, tb):
    c_len, heads, dk, dv = GLA_CHUNK, GLA_HEADS, GLA_DK_H, GLA_DV_H
    dk_all, dv_all = heads * dk, heads * dv
    q0, k0, v0, g0, a0 = 0, dk_all, 2 * dk_all, 2 * dk_all + dv_all, 2 * dk_all + 2 * dv_all

    row = lax.broadcasted_iota(jnp.int32, (c_len, c_len), 0)
    col = lax.broadcasted_iota(jnp.int32, (c_len, c_len), 1)
    causal = col <= row
    tril = jnp.where(causal, 1.0, 0.0).astype(BF16)
    scale = dk ** -0.5

    def projection(load_rows, slot):
        def norm():
            xn_ref[...] = _rms(load_rows(), nw_ref[...]).astype(BF16)

        def proj_q():
            q_buf[slot] = _dot(xn_ref[...], win_ref[:, q0:k0])

        def proj_k():
            k_buf[slot] = _dot(xn_ref[...], win_ref[:, k0:v0])

        def proj_v():
            v_buf[slot] = _dot(xn_ref[...], win_ref[:, v0:g0]).astype(BF16)

        def proj_g():
            g_buf[slot] = _dot(xn_ref[...], win_ref[:, g0:a0])

        def gate():
            a_hi, a_lo = _split_bf16(_dot(xn_ref[...], win_ref[:, a0:a0 + LANES]), 2)
            z = _dot(a_hi, wgb_ref[0]) + (_dot(a_hi, wgb_ref[1]) + _dot(a_lo, wgb_ref[0])) + bg_ref[...]
            la_buf[slot] = (jnp.minimum(z, 0.0) - jnp.log(1.0 + jnp.exp(-jnp.abs(z)))) / GLA_GATE_TEMP

        return [norm, proj_q, proj_k, proj_v, proj_g, gate]

    def chunk(c, slot):
        rows = pl.ds(c * c_len, c_len)
        b = None
        for p in _split_bf16(la_buf[slot, rows, :], 3):
            t = _dot(tril, p)
            b = t if b is None else b + t
        b_last = b[c_len - 1:c_len, :]
        q = q_buf[slot, rows, :]
        k = k_buf[slot, rows, :]
        q_dec = (q * scale * jnp.exp(b)).astype(BF16)
        k_inv = (k * jnp.exp(-b)).astype(BF16)
        k_end = (k * jnp.exp(b_last - b)).astype(BF16)
        decay = jnp.exp(b_last)
        v = v_buf[slot, rows, :]
        outs = []
        for h in range(heads):
            ks = slice(h * dk, (h + 1) * dk)
            vh = v[:, h * dv:(h + 1) * dv]
            qd = q_dec[:, ks]
            attn = lax.dot_general(qd, k_inv[:, ks], NT_DIMS, preferred_element_type=F32)
            attn = jnp.where(causal, attn, 0.0).astype(BF16)
            st = st_ref[h]
            o = _dot(attn, vh) + lax.dot_general(qd, st.astype(BF16), NT_DIMS,
                                                 preferred_element_type=F32)
            st_ref[h] = st * decay[:, ks] + lax.dot_general(vh, k_end[:, ks], TN_DIMS,
                                                            preferred_element_type=F32)
            outs.append(o * lax.rsqrt(jnp.mean(o * o, axis=-1, keepdims=True) + EPS))
        o = jnp.concatenate(outs, axis=-1)
        o = o * gnw_ref[...] * _silu(g_buf[slot, rows, :])
        og_buf[slot, rows, :] = o.astype(BF16)

    def run_block(slot, out_rows, fill_next):
        n_chunks = tb // c_len
        for c in range(n_chunks):
            chunk(c, slot)
            if c < len(fill_next):
                fill_next[c]()
        for step in fill_next[n_chunks:]:
            step()
        o_ref[0, out_rows, :] = h_ref[0, out_rows, :] + _dot(og_buf[slot], wout_ref[...])

    @pl.when(pl.program_id(1) == 0)
    def _():
        st_ref[...] = jnp.zeros_like(st_ref)
        for step in projection(lambda: h_ref[0, 0:tb, :], 0):
            step()

    run_block(0, pl.ds(0, tb), projection(lambda: h_ref[0, tb:2 * tb, :], 1))
    run_block(1, pl.ds(tb, tb), projection(lambda: hn_ref[0], 0))


def _gla_mixer(h, bsz, mix_norm, w_in, w_gate_b, b_gate, gn_w, w_out, tb=512):
    t, d = h.shape
    s = t // bsz
    dk_all, dv_all = GLA_HEADS * GLA_DK_H, GLA_HEADS * GLA_DV_H
    win = _pad_cols(w_in, 2 * dk_all + 2 * dv_all + LANES).astype(BF16)
    wgb = jnp.pad(w_gate_b, ((0, LANES - GLA_GATE_RANK), (0, 0)))
    wgb_hi = wgb.astype(BF16)
    wgb = jnp.stack([wgb_hi, (wgb - wgb_hi.astype(F32)).astype(BF16)])
    wout = w_out.astype(BF16)
    last_blk = s // tb - 1
    h3 = h.reshape(bsz, s, d)
    out = pl.pallas_call(
        functools.partial(_gla_kernel, tb=tb),
        grid=(bsz, s // (2 * tb)),
        in_specs=[pl.BlockSpec((1, 2 * tb, d), lambda b, i: (b, i, 0)),
                  pl.BlockSpec((1, tb, d), lambda b, i: (b, jnp.minimum(2 * i + 2, last_blk), 0)),
                  _const_spec((1, d)), _const_spec(win.shape), _const_spec(wgb.shape),
                  _const_spec((1, dk_all)), _const_spec((1, dv_all)), _const_spec(wout.shape)],
        out_specs=pl.BlockSpec((1, 2 * tb, d), lambda b, i: (b, i, 0)),
        out_shape=jax.ShapeDtypeStruct((bsz, s, d), F32),
        scratch_shapes=[pltpu.VMEM((GLA_HEADS, GLA_DV_H, GLA_DK_H), F32),
                        pltpu.VMEM((tb, d), BF16),
                        pltpu.VMEM((2, tb, dv_all), BF16),
                        pltpu.VMEM((2, tb, dk_all), F32), pltpu.VMEM((2, tb, dk_all), F32),
                        pltpu.VMEM((2, tb, dv_all), BF16), pltpu.VMEM((2, tb, dv_all), F32),
                        pltpu.VMEM((2, tb, dk_all), F32)],
        compiler_params=_params("parallel", "arbitrary"),
        name="gla",
    )(h3, h3, mix_norm.reshape(1, d), win, wgb, b_gate.reshape(1, dk_all), gn_w.reshape(1, dv_all), wout)
    return out.reshape(t, d)


def _ssd_kernel(h_ref, hn_ref, mnw_ref, win_ref, cw_ref, cb_ref, dtb_ref, alog_ref, dsk_ref, nw_ref, e_ref,
                wout_ref, o_ref, st_ref, xn_ref, tail_ref, conv_ref, z_buf, xbc_buf, dt_buf, y_buf, *, tb):
    q_len, n_grp, hpg, p_dim, n_st = SSD_CHUNK, SSD_GROUPS, SSD_HPG, SSD_HEAD_DIM, SSD_STATE
    gw = hpg * p_dim
    z0, x0, d0 = 0, SSD_D_INNER, SSD_D_INNER + SSD_CONV_DIM
    conv_piece = SSD_CONV_DIM // 3

    def projection(load_rows, slot):
        def norm():
            xn_ref[...] = _rms(load_rows(), mnw_ref[...]).astype(BF16)

        def proj_z(lo, hi):
            z_buf[slot, :, lo:hi] = _dot(xn_ref[...], win_ref[:, z0 + lo:z0 + hi])

        def proj_x(lo, hi):
            xbc_buf[slot, :, lo:hi] = _dot(xn_ref[...], win_ref[:, x0 + lo:x0 + hi])

        def proj_dt():
            dt_buf[slot] = _dot(xn_ref[...], win_ref[:, d0:d0 + LANES])

        half = SSD_D_INNER // 2
        steps = [norm, functools.partial(proj_z, 0, half), functools.partial(proj_z, half, SSD_D_INNER)]
        steps += [functools.partial(proj_x, j * conv_piece, (j + 1) * conv_piece) for j in range(3)]
        return steps + [proj_dt]

    def conv_step(slot, lo, hi):
        u = jnp.concatenate([tail_ref[:, lo:hi], xbc_buf[slot, :, lo:hi]], axis=0)
        tiles = u.reshape((tb + SSD_HALO) // SSD_HALO, SSD_HALO, hi - lo)
        sub = lax.broadcasted_iota(jnp.int32, (tb // SSD_HALO, SSD_HALO, hi - lo), 1)
        acc = cb_ref[:, lo:hi] + cw_ref[SSD_CONV - 1:SSD_CONV, lo:hi] * tiles[1:]
        for delay in range(1, SSD_CONV):
            rot = pltpu.roll(tiles, delay, 1)
            acc = acc + cw_ref[SSD_CONV - 1 - delay:SSD_CONV - delay, lo:hi] * jnp.where(
                sub >= delay, rot[1:], rot[:-1])
        conv_ref[:, lo:hi] = _silu(acc).reshape(tb, hi - lo)
        tail_ref[:, lo:hi] = xbc_buf[slot, tb - SSD_HALO:tb, lo:hi]

    row = lax.broadcasted_iota(jnp.int32, (q_len, q_len), 0)
    col = lax.broadcasted_iota(jnp.int32, (q_len, q_len), 1)
    causal = col <= row
    tril = jnp.where(causal, 1.0, 0.0).astype(BF16)
    er = lax.broadcasted_iota(jnp.int32, (LANES, LANES), 0)
    ec = lax.broadcasted_iota(jnp.int32, (LANES, LANES), 1)
    eye = jnp.where(er == ec, 1.0, 0.0).astype(BF16)
    lane = lax.broadcasted_iota(jnp.int32, (q_len, LANES), 1)
    low_half = lane < p_dim
    a_neg = -jnp.exp(alog_ref[...])

    def chunk_steps(c, slot):
        rows = pl.ds(c * q_len, q_len)
        shared = {}

        def decays():
            dt = _softplus(dt_buf[slot, rows, :] + dtb_ref[...])
            da = dt * a_neg
            a_cs = None
            for p in _split_bf16(da, 3):
                t = _dot(tril, p)
                a_cs = t if a_cs is None else a_cs + t
            a_last = a_cs[q_len - 1:q_len, :]
            a_cs_t = _transpose_via_mxu(eye, a_cs)
            shared["shift_t"] = a_cs_t - jnp.log(_transpose_via_mxu(eye, dt))
            e_mat = e_ref[...]
            shared["a_cs"] = a_cs
            shared["ea_x"] = _dot_exact_rhs(jnp.exp(a_cs), e_mat, 2)
            shared["w_x"] = _dot_exact_rhs(jnp.exp(a_last - a_cs) * dt, e_mat, 2)

        def group(g):
            a_cs, shift_t, ea_x, w_x = shared["a_cs"], shared["shift_t"], shared["ea_x"], shared["w_x"]
            tot_x = ea_x[q_len - 1:q_len, :]
            gs = slice(g * gw, (g + 1) * gw)
            x_g = conv_ref[rows, gs]
            b_g = conv_ref[rows, SSD_D_INNER + g * n_st:SSD_D_INNER + (g + 1) * n_st].astype(BF16)
            c_g = conv_ref[rows, SSD_D_INNER + (n_grp + g) * n_st:
                           SSD_D_INNER + (n_grp + g + 1) * n_st].astype(BF16)
            cb = lax.dot_general(c_g, b_g, NT_DIMS, preferred_element_type=F32)
            st = st_ref[g]
            y_off = _dot(c_g, st.astype(BF16)) * ea_x[:, gs]
            xw = (x_g * w_x[:, gs]).astype(BF16)
            st_ref[g] = st * tot_x[:, gs] + lax.dot_general(b_g, xw, TN_DIMS, preferred_element_type=F32)
            xb = x_g.astype(BF16)
            ys = []
            for pr in range(hpg // 2):
                h1 = g * hpg + 2 * pr
                x_pair = xb[:, pr * LANES:(pr + 1) * LANES]
                y_pair = None
                for hh, keep in ((h1, low_half), (h1 + 1, jnp.logical_not(low_half))):
                    seg = a_cs[:, hh:hh + 1] - shift_t[hh:hh + 1, :]
                    m = cb * jnp.where(causal, jnp.exp(seg), 0.0)
                    t = _dot(m.astype(BF16), jnp.where(keep, x_pair, jnp.zeros_like(x_pair)))
                    y_pair = t if y_pair is None else y_pair + t
                ys.append(y_pair)
            y = jnp.concatenate(ys, axis=-1) + y_off + x_g * dsk_ref[:, gs]
            y = y * _silu(z_buf[slot, rows, gs])
            y = y * lax.rsqrt(jnp.mean(y * y, axis=-1, keepdims=True) + EPS) * nw_ref[:, gs]
            y_buf[slot, rows, gs] = y.astype(BF16)

        return [decays] + [functools.partial(group, g) for g in range(n_grp)]

    def run_block(slot, out_rows, fill_next):
        work = [functools.partial(conv_step, slot, j * conv_piece, (j + 1) * conv_piece) for j in range(3)]
        for c in range(tb // q_len):
            work += chunk_steps(c, slot)
        for j, step in enumerate(work):
            step()
            if j < len(fill_next):
                fill_next[j]()
        for step in fill_next[len(work):]:
            step()
        o_ref[0, out_rows, :] = h_ref[0, out_rows, :] + _dot(y_buf[slot], wout_ref[...])

    @pl.when(pl.program_id(1) == 0)
    def _():
        st_ref[...] = jnp.zeros_like(st_ref)
        tail_ref[...] = jnp.zeros_like(tail_ref)
        for step in projection(lambda: h_ref[0, 0:tb, :], 0):
            step()

    run_block(0, pl.ds(0, tb), projection(lambda: h_ref[0, tb:2 * tb, :], 1))
    run_block(1, pl.ds(tb, tb), projection(lambda: hn_ref[0], 0))


def _ssd_mixer(h, bsz, mix_norm, w_in, conv_w, conv_b, dt_bias, a_log, d_skip, norm_w, w_out, tb=256):
    t, d = h.shape
    s = t // bsz
    win = _pad_cols(w_in, SSD_D_INNER + SSD_CONV_DIM + LANES).astype(BF16)
    wout = w_out.astype(BF16)
    head_of_lane = jnp.arange(SSD_D_INNER) // SSD_HEAD_DIM
    expand = (jnp.arange(LANES)[:, None] == head_of_lane[None, :]).astype(BF16)

    def pad_row(v):
        return jnp.pad(v, (0, LANES - v.shape[0])).reshape(1, LANES)

    last_blk = s // tb - 1
    h3 = h.reshape(bsz, s, d)
    out = pl.pallas_call(
        functools.partial(_ssd_kernel, tb=tb),
        grid=(bsz, s // (2 * tb)),
        in_specs=[pl.BlockSpec((1, 2 * tb, d), lambda b, i: (b, i, 0)),
                  pl.BlockSpec((1, tb, d), lambda b, i: (b, jnp.minimum(2 * i + 2, last_blk), 0)),
                  _const_spec((1, d)), _const_spec(win.shape),
                  _const_spec((SSD_CONV, SSD_CONV_DIM)), _const_spec((1, SSD_CONV_DIM)),
                  _const_spec((1, LANES)), _const_spec((1, LANES)),
                  _const_spec((1, SSD_D_INNER)), _const_spec((1, SSD_D_INNER)),
                  _const_spec((LANES, SSD_D_INNER)), _const_spec(wout.shape)],
        out_specs=pl.BlockSpec((1, 2 * tb, d), lambda b, i: (b, i, 0)),
        out_shape=jax.ShapeDtypeStruct((bsz, s, d), F32),
        scratch_shapes=[pltpu.VMEM((SSD_GROUPS, SSD_STATE, SSD_HPG * SSD_HEAD_DIM), F32),
                        pltpu.VMEM((tb, d), BF16),
                        pltpu.VMEM((SSD_HALO, SSD_CONV_DIM), F32),
                        pltpu.VMEM((tb, SSD_CONV_DIM), F32),
                        pltpu.VMEM((2, tb, SSD_D_INNER), F32),
                        pltpu.VMEM((2, tb, SSD_CONV_DIM), F32),
                        pltpu.VMEM((2, tb, LANES), F32),
                        pltpu.VMEM((2, tb, SSD_D_INNER), BF16)],
        compiler_params=_params("parallel", "arbitrary"),
        name="ssd",
    )(h3, h3, mix_norm.reshape(1, d), win, conv_w, conv_b.reshape(1, -1), pad_row(dt_bias), pad_row(a_log),
      jnp.repeat(d_skip, SSD_HEAD_DIM).reshape(1, -1), norm_w.reshape(1, -1), expand, wout)
    return out.reshape(t, d)


def _mla_qkv_kernel(h_ref, pos_ref, nw_ref, win_ref, qn_ref, kvn_ref, wuq_ref, wuk_ref, wuv_ref, freq_ref,
                    q_ref, k_ref, v_ref):
    half = MLA_ROPE // 2
    o1, o2 = MLA_Q_LORA, MLA_Q_LORA + MLA_KV_LORA
    lat = _dot(_rms(h_ref[...], nw_ref[...]).astype(BF16), win_ref[...])
    cqn = _rms(lat[:, :o1], qn_ref[...]).astype(BF16)
    ckvn = _rms(lat[:, o1:o2], kvn_ref[...]).astype(BF16)
    tm = pos_ref.shape[0]
    lane = lax.broadcasted_iota(jnp.int32, (tm, LANES), 1)
    lane_h = lax.broadcasted_iota(jnp.int32, (tm // 2, LANES), 1)
    freq = freq_ref[...]
    ang2 = jnp.where(lane_h < MLA_ROPE, pos_ref[0:tm // 2, :] * freq, pos_ref[tm // 2:tm, :] * freq)
    cos2, sin2 = jnp.cos(ang2), jnp.sin(ang2)
    cos_a = jnp.concatenate([cos2, pltpu.roll(cos2, MLA_ROPE, 1)], axis=0)
    sin_a = jnp.concatenate([sin2, pltpu.roll(sin2, MLA_ROPE, 1)], axis=0)
    cos_v = jnp.where(lane < MLA_ROPE, cos_a, 0.0)
    sin_v = jnp.where(lane < half, -sin_a, jnp.where(lane < MLA_ROPE, sin_a, 0.0))

    def rope(t):
        return t * cos_v + pltpu.roll(t, MLA_ROPE, 1) * sin_v

    scale = MLA_QK ** -0.5 * math.log2(math.e)
    q = _dot(cqn, wuq_ref[...])
    k_rope = rope(lat[:, o2:o2 + LANES]).astype(BF16)
    k_nope = _dot(ckvn, wuk_ref[...])
    for h in range(MLA_HEADS):
        base = h * MLA_QK_PAD
        q_ref[:, base:base + MLA_NOPE] = (q[:, base:base + MLA_NOPE] * scale).astype(BF16)
        q_ref[:, base + MLA_NOPE:base + MLA_QK_PAD] = (
            rope(q[:, base + MLA_NOPE:base + MLA_QK_PAD]) * scale).astype(BF16)
        k_ref[:, base:base + MLA_NOPE] = k_nope[:, h * MLA_NOPE:(h + 1) * MLA_NOPE].astype(BF16)
        k_ref[:, base + MLA_NOPE:base + MLA_QK_PAD] = k_rope
    v = _dot(ckvn, wuv_ref[...])
    ones_col = jnp.where(lane == 0, 1.0, 0.0).astype(BF16)
    for h in range(MLA_HEADS):
        base = h * MLA_V_PAD
        v_ref[:, base:base + MLA_V] = v[:, h * MLA_V:(h + 1) * MLA_V].astype(BF16)
        v_ref[:, base + MLA_V:base + MLA_V_PAD] = ones_col


def _flash_kernel(q_ref, k_ref, v_ref, o_ref, acc_ref, *, tq, hps):
    qi = pl.program_id(2)
    acc_ref[...] = jnp.zeros_like(acc_ref)
    row = lax.broadcasted_iota(jnp.int32, (tq, tq), 0)
    col = lax.broadcasted_iota(jnp.int32, (tq, tq), 1)

    def step(blocks, ms, mask_last):
        key_rows = [pl.ds(pl.multiple_of(j * tq, tq), tq) for j in blocks]
        new_ms = []
        for hd in range(hps):
            q = q_ref[0, :, hd * MLA_QK_PAD:(hd + 1) * MLA_QK_PAD]
            scores = []
            for n, ks in enumerate(key_rows):
                k = k_ref[0, ks, hd * MLA_QK_PAD:(hd + 1) * MLA_QK_PAD]
                s = lax.dot_general(q, k, NT_DIMS, preferred_element_type=F32)
                if mask_last and n == len(key_rows) - 1:
                    s = jnp.where(col <= row, s, -jnp.inf)
                scores.append(s)
            m_new = ms[hd]
            for s in scores:
                m_new = jnp.maximum(m_new, jnp.max(s, axis=-1, keepdims=True))
            pv = None
            for s, ks in zip(scores, key_rows):
                p = jnp.exp2((s - m_new).astype(BF16))
                t = _dot(p, v_ref[0, ks, hd * MLA_V_PAD:(hd + 1) * MLA_V_PAD])
                pv = t if pv is None else pv + t
            acc_ref[hd] = jnp.exp2(ms[hd] - m_new) * acc_ref[hd] + pv
            new_ms.append(m_new)
        return tuple(new_ms)

    init = tuple(jnp.full((tq, 1), -jnp.inf, F32) for _ in range(hps))
    ms = lax.fori_loop(0, qi // 2, lambda j, cr: step([2 * j, 2 * j + 1], cr, False), init)
    lax.cond(qi % 2 == 1,
             lambda cr: step([qi - 1, qi], cr, True),
             lambda cr: step([qi], cr, True),
             ms)
    for hd in range(hps):
        acc = acc_ref[hd]
        o_ref[0, :, hd * MLA_V:(hd + 1) * MLA_V] = (acc[:, :MLA_V] / acc[:, MLA_V:MLA_V + 1]).astype(o_ref.dtype)


def _mla_mixer(h, bsz, positions, mix_norm, w_in, q_norm, w_uq, kv_norm, w_ukv, w_out, tm=512, tq=512):
    t, d = h.shape
    s = t // bsz
    half = MLA_ROPE // 2

    def swap_halves(w):
        return jnp.concatenate([w[..., half:], w[..., :half]], axis=-1)

    o1, o2 = MLA_Q_LORA, MLA_Q_LORA + MLA_KV_LORA
    win = jnp.concatenate([w_in, swap_halves(w_in[:, o2:])], axis=-1).astype(BF16)

    wq = w_uq.reshape(MLA_Q_LORA, MLA_HEADS, MLA_QK)
    wq_rope = wq[..., MLA_NOPE:]
    wuq = jnp.concatenate([wq[..., :MLA_NOPE], wq_rope, swap_halves(wq_rope)], axis=-1)
    wuq = wuq.reshape(MLA_Q_LORA, MLA_HEADS * MLA_QK_PAD).astype(BF16)
    wkv = w_ukv.reshape(MLA_KV_LORA, MLA_HEADS, MLA_NOPE + MLA_V)
    wuk = wkv[..., :MLA_NOPE].reshape(MLA_KV_LORA, MLA_HEADS * MLA_NOPE).astype(BF16)
    wuv = wkv[..., MLA_NOPE:].reshape(MLA_KV_LORA, MLA_HEADS * MLA_V).astype(BF16)
    inv_freq = 1.0 / (ROPE_THETA ** (jnp.arange(0, MLA_ROPE, 2, dtype=F32) / MLA_ROPE))
    freq = jnp.tile(inv_freq, LANES // half).reshape(1, LANES)
    pos = positions.astype(F32).reshape(t, 1)

    def rows(n):
        return pl.BlockSpec((tm, n), lambda i: (i, 0))

    qk_w = MLA_HEADS * MLA_QK_PAD
    v_w = MLA_HEADS * MLA_V
    vp_w = MLA_HEADS * MLA_V_PAD
    qf, kf, vf = pl.pallas_call(
        _mla_qkv_kernel,
        grid=(t // tm,),
        in_specs=[rows(d), rows(1), _const_spec((1, d)), _const_spec(win.shape),
                  _const_spec((1, MLA_Q_LORA)), _const_spec((1, MLA_KV_LORA)),
                  _const_spec(wuq.shape), _const_spec(wuk.shape), _const_spec(wuv.shape),
                  _const_spec((1, LANES))],
        out_specs=[rows(qk_w), rows(qk_w), rows(vp_w)],
        out_shape=[jax.ShapeDtypeStruct((t, qk_w), BF16), jax.ShapeDtypeStruct((t, qk_w), BF16),
                   jax.ShapeDtypeStruct((t, vp_w), BF16)],
        compiler_params=_params("parallel"),
        name="mla_qkv",
    )(h, pos, mix_norm.reshape(1, d), win, q_norm.reshape(1, -1), kv_norm.reshape(1, -1), wuq, wuk, wuv, freq)

    hps = FLASH_HEADS_PER_STEP
    o = pl.pallas_call(
        functools.partial(_flash_kernel, tq=tq, hps=hps),
        grid=(bsz, MLA_HEADS // hps, s // tq),
        in_specs=[pl.BlockSpec((1, tq, hps * MLA_QK_PAD), lambda b, hd, i: (b, i, hd)),
                  pl.BlockSpec((1, s, hps * MLA_QK_PAD), lambda b, hd, i: (b, 0, hd)),
                  pl.BlockSpec((1, s, hps * MLA_V_PAD), lambda b, hd, i: (b, 0, hd))],
        out_specs=pl.BlockSpec((1, tq, hps * MLA_V), lambda b, hd, i: (b, i, hd)),
        out_shape=jax.ShapeDtypeStruct((bsz, s, v_w), BF16),
        scratch_shapes=[pltpu.VMEM((hps, tq, MLA_V_PAD), F32)],
        compiler_params=_params("parallel", "parallel", "arbitrary"),
        name="mla_flash",
    )(qf.reshape(bsz, s, qk_w), kf.reshape(bsz, s, qk_w), vf.reshape(bsz, s, vp_w))
    return o.reshape(t, v_w), w_out.astype(BF16)


def kernel(x, positions, l0_ffn1_norm, l0_ffn1_w_gu, l0_ffn1_w_down, l0_mix_norm, l0_gla_w_in, l0_gla_w_gate_b, l0_gla_b_gate, l0_gla_norm, l0_gla_w_out, l0_ffn2_norm, l0_ffn2_w_gu, l0_ffn2_w_down, l1_ffn1_norm, l1_ffn1_w_gu, l1_ffn1_w_down, l1_mix_norm, l1_ssd_w_in, l1_ssd_conv_w, l1_ssd_conv_b, l1_ssd_dt_bias, l1_ssd_a_log, l1_ssd_d_skip, l1_ssd_norm, l1_ssd_w_out, l1_ffn2_norm, l1_ffn2_w_gu, l1_ffn2_w_down, l2_ffn1_norm, l2_ffn1_w_gu, l2_ffn1_w_down, l2_mix_norm, l2_mla_w_in, l2_mla_q_norm, l2_mla_w_uq, l2_mla_kv_norm, l2_mla_w_ukv, l2_mla_w_out, l2_ffn2_norm, l2_ffn2_w_gu, l2_ffn2_w_down, l3_ffn1_norm, l3_ffn1_w_gu, l3_ffn1_w_down, l3_mix_norm, l3_gla_w_in, l3_gla_w_gate_b, l3_gla_b_gate, l3_gla_norm, l3_gla_w_out, l3_ffn2_norm, l3_ffn2_w_gu, l3_ffn2_w_down, final_norm):
    bsz, s, d = x.shape
    h = x.reshape(bsz * s, d)

    h = _ffn(h, l0_ffn1_norm, l0_ffn1_w_gu, l0_ffn1_w_down)
    h = _gla_mixer(h, bsz, l0_mix_norm, l0_gla_w_in, l0_gla_w_gate_b, l0_gla_b_gate, l0_gla_norm, l0_gla_w_out)
    h = _ffn(h, l0_ffn2_norm, l0_ffn2_w_gu, l0_ffn2_w_down)

    h = _ffn(h, l1_ffn1_norm, l1_ffn1_w_gu, l1_ffn1_w_down)
    h = _ssd_mixer(h, bsz, l1_mix_norm, l1_ssd_w_in, l1_ssd_conv_w, l1_ssd_conv_b, l1_ssd_dt_bias,
                   l1_ssd_a_log, l1_ssd_d_skip, l1_ssd_norm, l1_ssd_w_out)
    h = _ffn(h, l1_ffn2_norm, l1_ffn2_w_gu, l1_ffn2_w_down)

    h = _ffn(h, l2_ffn1_norm, l2_ffn1_w_gu, l2_ffn1_w_down)
    attn = _mla_mixer(h, bsz, positions, l2_mix_norm, l2_mla_w_in, l2_mla_q_norm, l2_mla_w_uq,
                      l2_mla_kv_norm, l2_mla_w_ukv, l2_mla_w_out)
    h = _ffn(h, l2_ffn2_norm, l2_ffn2_w_gu, l2_ffn2_w_down, mixer_out=attn)

    h = _ffn(h, l3_ffn1_norm, l3_ffn1_w_gu, l3_ffn1_w_down)
    h = _gla_mixer(h, bsz, l3_mix_norm, l3_gla_w_in, l3_gla_w_gate_b, l3_gla_b_gate, l3_gla_norm, l3_gla_w_out)
    h = _ffn(h, l3_ffn2_norm, l3_ffn2_w_gu, l3_ffn2_w_down, final_w=final_norm)
    return h.reshape(bsz, s, d)
```

```python
import functools
import math

import jax
import jax.numpy as jnp
from jax import lax
from jax.experimental import pallas as pl
from jax.experimental.pallas import tpu as pltpu

F32 = jnp.float32
BF16 = jnp.bfloat16
EPS = 1e-6

V7X_VMEM_BYTES = 64 * 1024 * 1024
VMEM_LIMIT = V7X_VMEM_BYTES - 8 * 1024 * 1024
LANES = 128

D_FF = 2816
FF_CHUNK = 256

GLA_HEADS = 4
GLA_DK_H = 128
GLA_DV_H = 256
GLA_GATE_RANK = 16
GLA_GATE_TEMP = 16.0
GLA_CHUNK = 64

SSD_D_INNER = 2048
SSD_HEAD_DIM = 64
SSD_HEADS = 32
SSD_GROUPS = 4
SSD_HPG = 8
SSD_STATE = 128
SSD_CONV = 4
SSD_CHUNK = 128
SSD_CONV_DIM = 3072
SSD_HALO = 8

MLA_HEADS = 8
MLA_Q_LORA = 384
MLA_KV_LORA = 256
MLA_NOPE = 128
MLA_ROPE = 64
MLA_V = 128
MLA_QK = MLA_NOPE + MLA_ROPE
MLA_QK_PAD = 256
ROPE_THETA = 10000.0
MLA_V_PAD = 256
FLASH_HEADS_PER_STEP = 4

NT_DIMS = (((1,), (1,)), ((), ()))
TN_DIMS = (((0,), (0,)), ((), ()))


def _params(*sem):
    return pltpu.CompilerParams(dimension_semantics=sem, vmem_limit_bytes=VMEM_LIMIT)


def _const_spec(shape):
    return pl.BlockSpec(shape, lambda *_: (0,) * len(shape), pipeline_mode=pl.Buffered(1))


def _rms(x, w):
    return x * lax.rsqrt(jnp.mean(x * x, axis=-1, keepdims=True) + EPS) * w


def _silu(x):
    return x / (1.0 + jnp.exp(-x))


def _softplus(x):
    return jnp.maximum(x, 0.0) + jnp.log(1.0 + jnp.exp(-jnp.abs(x)))


def _dot(a, b):
    return jnp.dot(a, b, preferred_element_type=F32)


def _split_bf16(a, terms):
    parts = []
    for _ in range(terms):
        p = a.astype(BF16)
        parts.append(p)
        a = a - p.astype(F32)
    return parts


def _dot_exact_rhs(a, b, terms, dims=None):
    out = None
    for p in _split_bf16(a, terms):
        if dims is None:
            t = _dot(p, b)
        else:
            t = lax.dot_general(p, b, dims, preferred_element_type=F32)
        out = t if out is None else out + t
    return out


def _transpose_via_mxu(eye, a, terms=2):
    out = None
    for p in _split_bf16(a, terms):
        t = lax.dot_general(eye, p, NT_DIMS, preferred_element_type=F32)
        out = t if out is None else out + t
    return out


def _ffn_kernel(*refs, n_chunks, tf, final, mixer_out):
    refs = list(refs)
    x_ref = refs.pop(0)
    if mixer_out:
        a_ref, wo_ref = refs.pop(0), refs.pop(0)
    nw_ref, wgu_ref, wd_ref = refs.pop(0), refs.pop(0), refs.pop(0)
    if final:
        fw_ref = refs.pop(0)
    o_ref, xn_ref, acc_ref = refs[:3]
    if mixer_out:
        x_in = refs[3]
        x_in[...] = x_ref[...] + _dot(a_ref[...], wo_ref[...])
    else:
        x_in = x_ref
    x_ref = x_in
    d_ff = n_chunks * tf
    xn_ref[...] = _rms(x_ref[...], nw_ref[...]).astype(BF16)

    def down(c):
        if isinstance(c, int):
            g_cols, u_cols = pl.ds(c * tf, tf), pl.ds(d_ff + c * tf, tf)
        else:
            g_cols = pl.ds(pl.multiple_of(c * tf, tf), tf)
            u_cols = pl.ds(pl.multiple_of(d_ff + c * tf, tf), tf)
        xn = xn_ref[...]
        a = (_silu(_dot(xn, wgu_ref[:, g_cols])) * _dot(xn, wgu_ref[:, u_cols])).astype(BF16)
        return _dot(a, wd_ref[g_cols, :])

    acc_ref[...] = down(0)

    for c in range(1, n_chunks - 1):
        acc_ref[...] += down(c)
    y = x_ref[...] + (acc_ref[...] + down(n_chunks - 1))
    if final:
        y = _rms(y, fw_ref[...])
    o_ref[...] = y


def _ffn(h, norm_w, w_gu, w_down, final_w=None, mixer_out=None, tm=1024):
    t, d = h.shape
    n_chunks = D_FF // FF_CHUNK
    wgu = w_gu.astype(BF16)
    wd = (0.5 * w_down).astype(BF16)
    args = [h]
    in_specs = [pl.BlockSpec((tm, d), lambda i: (i, 0))]
    scratch = [pltpu.VMEM((tm, d), BF16), pltpu.VMEM((tm, d), F32)]
    if mixer_out is not None:
        a, w_out = mixer_out
        args += [a, w_out]
        in_specs += [pl.BlockSpec((tm, a.shape[1]), lambda i: (i, 0)), _const_spec(w_out.shape)]
        scratch.append(pltpu.VMEM((tm, d), F32))
    args += [norm_w.reshape(1, d), wgu, wd]
    in_specs += [_const_spec((1, d)), _const_spec(wgu.shape), _const_spec(wd.shape)]
    if final_w is not None:
        args.append(final_w.reshape(1, d))
        in_specs.append(_const_spec((1, d)))
    return pl.pallas_call(
        functools.partial(_ffn_kernel, n_chunks=n_chunks, tf=FF_CHUNK, final=final_w is not None,
                          mixer_out=mixer_out is not None),
        grid=(t // tm,),
        in_specs=in_specs,
        out_specs=pl.BlockSpec((tm, d), lambda i: (i, 0)),
        out_shape=jax.ShapeDtypeStruct((t, d), F32),
        scratch_shapes=scratch,
        compiler_params=_params("parallel"),
        name="ffn",
    )(*args)


def _pad_cols(w, n):
    return jnp.pad(w, ((0, 0), (0, n - w.shape[1])))


def _gla_kernel(h_ref, hn_ref, nw_ref, win_ref, wgb_ref, bg_ref, gnw_ref, wout_ref, o_ref,
                st_ref, xn_ref, og_buf, q_buf, k_buf, v_buf, g_buf, la_buf, *, tb):
    c_len, heads, dk, dv = GLA_CHUNK, GLA_HEADS, GLA_DK_H, GLA_DV_H
    dk_all, dv_all = heads * dk, heads * dv
    q0, k0, v0, g0, a0 = 0, dk_all, 2 * dk_all, 2 * dk_all + dv_all, 2 * dk_all + 2 * dv_all

    row = lax.broadcasted_iota(jnp.int32, (c_len, c_len), 0)
    col = lax.broadcasted_iota(jnp.int32, (c_len, c_len), 1)
    causal = col <= row
    tril = jnp.where(causal, 1.0, 0.0).astype(BF16)
    scale = dk ** -0.5

    def projection(load_rows, slot):
        def norm():
            xn_ref[...] = _rms(load_rows(), nw_ref[...]).astype(BF16)

        def proj_q():
            q_buf[slot] = _dot(xn_ref[...], win_ref[:, q0:k0])

        def proj_k():
            k_buf[slot] = _dot(xn_ref[...], win_ref[:, k0:v0])

        def proj_v():
            v_buf[slot] = _dot(xn_ref[...], win_ref[:, v0:g0]).astype(BF16)

        def proj_g():
            g_buf[slot] = _dot(xn_ref[...], win_ref[:, g0:a0])

        def gate():
            a_hi, a_lo = _split_bf16(_dot(xn_ref[...], win_ref[:, a0:a0 + LANES]), 2)
            z = _dot(a_hi, wgb_ref[0]) + (_dot(a_hi, wgb_ref[1]) + _dot(a_lo, wgb_ref[0])) + bg_ref[...]
            la_buf[slot] = (jnp.minimum(z, 0.0) - jnp.log(1.0 + jnp.exp(-jnp.abs(z)))) / GLA_GATE_TEMP

        return [norm, proj_q, proj_k, proj_v, proj_g, gate]

    def chunk(c, slot):
        rows = pl.ds(c * c_len, c_len)
        b = None
        for p in _split_bf16(la_buf[slot, rows, :], 2):
            t = _dot(tril, p)
            b = t if b is None else b + t
        b_last = b[c_len - 1:c_len, :]
        q = q_buf[slot, rows, :]
        k = k_buf[slot, rows, :]
        q_dec = (q * scale * jnp.exp(b)).astype(BF16)
        k_inv = (k * jnp.exp(-b)).astype(BF16)
        k_end = (k * jnp.exp(b_last - b)).astype(BF16)
        decay = jnp.exp(b_last)
        v = v_buf[slot, rows, :]
        outs = []
        for h in range(heads):
            ks = slice(h * dk, (h + 1) * dk)
            vh = v[:, h * dv:(h + 1) * dv]
            qd = q_dec[:, ks]
            attn = lax.dot_general(qd, k_inv[:, ks], NT_DIMS, preferred_element_type=F32)
            attn = jnp.where(causal, attn, 0.0).astype(BF16)
            st = st_ref[h]
            o = _dot(attn, vh) + lax.dot_general(qd, st.astype(BF16), NT_DIMS,
                                                 preferred_element_type=F32)
            st_ref[h] = st * decay[:, ks] + lax.dot_general(vh, k_end[:, ks], TN_DIMS,
                                                            preferred_element_type=F32)
            outs.append(o * lax.rsqrt(jnp.mean(o * o, axis=-1, keepdims=True) + EPS))
        o = jnp.concatenate(outs, axis=-1)
        o = o * gnw_ref[...] * _silu(g_buf[slot, rows, :])
        og_buf[slot, rows, :] = o.astype(BF16)

    def run_block(slot, out_rows, fill_next):
        n_chunks = tb // c_len
        for c in range(n_chunks):
            chunk(c, slot)
            if c < len(fill_next):
                fill_next[c]()
        for step in fill_next[n_chunks:]:
            step()
        o_ref[0, out_rows, :] = h_ref[0, out_rows, :] + _dot(og_buf[slot], wout_ref[...])

    @pl.when(pl.program_id(1) == 0)
    def _():
        st_ref[...] = jnp.zeros_like(st_ref)
        for step in projection(lambda: h_ref[0, 0:tb, :], 0):
            step()

    run_block(0, pl.ds(0, tb), projection(lambda: h_ref[0, tb:2 * tb, :], 1))
    run_block(1, pl.ds(tb, tb), projection(lambda: hn_ref[0], 0))


def _gla_mixer(h, bsz, mix_norm, w_in, w_gate_b, b_gate, gn_w, w_out, tb=512):
    t, d = h.shape
    s = t // bsz
    dk_all, dv_all = GLA_HEADS * GLA_DK_H, GLA_HEADS * GLA_DV_H
    win = _pad_cols(w_in, 2 * dk_all + 2 * dv_all + LANES).astype(BF16)
    wgb = jnp.pad(w_gate_b, ((0, LANES - GLA_GATE_RANK), (0, 0)))
    wgb_hi = wgb.astype(BF16)
    wgb = jnp.stack([wgb_hi, (wgb - wgb_hi.astype(F32)).astype(BF16)])
    wout = w_out.astype(BF16)
    last_blk = s // tb - 1
    h3 = h.reshape(bsz, s, d)
    out = pl.pallas_call(
        functools.partial(_gla_kernel, tb=tb),
        grid=(bsz, s // (2 * tb)),
        in_specs=[pl.BlockSpec((1, 2 * tb, d), lambda b, i: (b, i, 0)),
                  pl.BlockSpec((1, tb, d), lambda b, i: (b, jnp.minimum(2 * i + 2, last_blk), 0)),
                  _const_spec((1, d)), _const_spec(win.shape), _const_spec(wgb.shape),
                  _const_spec((1, dk_all)), _const_spec((1, dv_all)), _const_spec(wout.shape)],
        out_specs=pl.BlockSpec((1, 2 * tb, d), lambda b, i: (b, i, 0)),
        out_shape=jax.ShapeDtypeStruct((bsz, s, d), F32),
        scratch_shapes=[pltpu.VMEM((GLA_HEADS, GLA_DV_H, GLA_DK_H), F32),
                        pltpu.VMEM((tb, d), BF16),
                        pltpu.VMEM((2, tb, dv_all), BF16),
                        pltpu.VMEM((2, tb, dk_all), F32), pltpu.VMEM((2, tb, dk_all), F32),
                        pltpu.VMEM((2, tb, dv_all), BF16), pltpu.VMEM((2, tb, dv_all), F32),
                        pltpu.VMEM((2, tb, dk_all), F32)],
        compiler_params=_params("parallel", "arbitrary"),
        name="gla",
    )(h3, h3, mix_norm.reshape(1, d), win, wgb, b_gate.reshape(1, dk_all), gn_w.reshape(1, dv_all), wout)
    return out.reshape(t, d)


def _ssd_kernel(h_ref, hn_ref, mnw_ref, win_ref, cw_ref, cb_ref, dtb_ref, alog_ref, dsk_ref, nw_ref, e_ref,
                wout_ref, o_ref, st_ref, xn_ref, tail_ref, conv_ref, z_buf, xbc_buf, dt_buf, y_buf, *, tb):
    q_len, n_grp, hpg, p_dim, n_st = SSD_CHUNK, SSD_GROUPS, SSD_HPG, SSD_HEAD_DIM, SSD_STATE
    gw = hpg * p_dim
    z0, x0, d0 = 0, SSD_D_INNER, SSD_D_INNER + SSD_CONV_DIM
    conv_piece = SSD_CONV_DIM // 3

    def projection(load_rows, slot):
        def norm():
            xn_ref[...] = _rms(load_rows(), mnw_ref[...]).astype(BF16)

        def proj_z(lo, hi):
            z_buf[slot, :, lo:hi] = _dot(xn_ref[...], win_ref[:, z0 + lo:z0 + hi])

        def proj_x(lo, hi):
            xbc_buf[slot, :, lo:hi] = _dot(xn_ref[...], win_ref[:, x0 + lo:x0 + hi])

        def proj_dt():
            dt_buf[slot] = _dot(xn_ref[...], win_ref[:, d0:d0 + LANES])

        half = SSD_D_INNER // 2
        steps = [norm, functools.partial(proj_z, 0, half), functools.partial(proj_z, half, SSD_D_INNER)]
        steps += [functools.partial(proj_x, j * conv_piece, (j + 1) * conv_piece) for j in range(3)]
        return steps + [proj_dt]

    def conv_step(slot, lo, hi):
        u = jnp.concatenate([tail_ref[:, lo:hi], xbc_buf[slot, :, lo:hi]], axis=0)
        tiles = u.reshape((tb + SSD_HALO) // SSD_HALO, SSD_HALO, hi - lo)
        sub = lax.broadcasted_iota(jnp.int32, (tb // SSD_HALO, SSD_HALO, hi - lo), 1)
        acc = cb_ref[:, lo:hi] + cw_ref[SSD_CONV - 1:SSD_CONV, lo:hi] * tiles[1:]
        for delay in range(1, SSD_CONV):
            rot = pltpu.roll(tiles, delay, 1)
            acc = acc + cw_ref[SSD_CONV - 1 - delay:SSD_CONV - delay, lo:hi] * jnp.where(
                sub >= delay, rot[1:], rot[:-1])
        conv_ref[:, lo:hi] = _silu(acc).reshape(tb, hi - lo)
        tail_ref[:, lo:hi] = xbc_buf[slot, tb - SSD_HALO:tb, lo:hi]

    row = lax.broadcasted_iota(jnp.int32, (q_len, q_len), 0)
    col = lax.broadcasted_iota(jnp.int32, (q_len, q_len), 1)
    causal = col <= row
    tril = jnp.where(causal, 1.0, 0.0).astype(BF16)
    er = lax.broadcasted_iota(jnp.int32, (LANES, LANES), 0)
    ec = lax.broadcasted_iota(jnp.int32, (LANES, LANES), 1)
    eye = jnp.where(er == ec, 1.0, 0.0).astype(BF16)
    lane = lax.broadcasted_iota(jnp.int32, (q_len, LANES), 1)
    low_half = lane < p_dim
    a_neg = -jnp.exp(alog_ref[...])

    def chunk_steps(c, slot):
        rows = pl.ds(c * q_len, q_len)
        shared = {}

        def decays():
            dt = _softplus(dt_buf[slot, rows, :] + dtb_ref[...])
            da = dt * a_neg
            a_cs = None
            for p in _split_bf16(da, 2):
                t = _dot(tril, p)
                a_cs = t if a_cs is None else a_cs + t
            a_last = a_cs[q_len - 1:q_len, :]
            a_cs_t = _transpose_via_mxu(eye, a_cs)
            shared["shift_t"] = a_cs_t - jnp.log(_transpose_via_mxu(eye, dt))
            e_mat = e_ref[...]
            shared["a_cs"] = a_cs
            both = jnp.concatenate([jnp.exp(a_cs), jnp.exp(a_last - a_cs) * dt], axis=0).astype(BF16)
            both_x = _dot(both, e_mat)
            shared["ea_x"] = both_x[:q_len]
            shared["w_x"] = both_x[q_len:]
            shared["tot_x"] = _dot_exact_rhs(jnp.exp(a_last), e_mat, 2)

        def group(g):
            a_cs, shift_t, ea_x, w_x = shared["a_cs"], shared["shift_t"], shared["ea_x"], shared["w_x"]
            tot_x = shared["tot_x"]
            gs = slice(g * gw, (g + 1) * gw)
            x_g = conv_ref[rows, gs]
            b_g = conv_ref[rows, SSD_D_INNER + g * n_st:SSD_D_INNER + (g + 1) * n_st].astype(BF16)
            c_g = conv_ref[rows, SSD_D_INNER + (n_grp + g) * n_st:
                           SSD_D_INNER + (n_grp + g + 1) * n_st].astype(BF16)
            cb = lax.dot_general(c_g, b_g, NT_DIMS, preferred_element_type=F32)
            st = st_ref[g]
            y_off = _dot(c_g, st.astype(BF16)) * ea_x[:, gs]
            xw = (x_g * w_x[:, gs]).astype(BF16)
            st_ref[g] = st * tot_x[:, gs] + lax.dot_general(b_g, xw, TN_DIMS, preferred_element_type=F32)
            xb = x_g.astype(BF16)
            ys = []
            for pr in range(hpg // 2):
                h1 = g * hpg + 2 * pr
                x_pair = xb[:, pr * LANES:(pr + 1) * LANES]
                y_pair = None
                for hh, keep in ((h1, low_half), (h1 + 1, jnp.logical_not(low_half))):
                    seg = a_cs[:, hh:hh + 1] - shift_t[hh:hh + 1, :]
                    m = cb * jnp.where(causal, jnp.exp(seg), 0.0)
                    t = _dot(m.astype(BF16), jnp.where(keep, x_pair, jnp.zeros_like(x_pair)))
                    y_pair = t if y_pair is None else y_pair + t
                ys.append(y_pair)
            y = jnp.concatenate(ys, axis=-1) + y_off + x_g * dsk_ref[:, gs]
            y = y * _silu(z_buf[slot, rows, gs])
            y = y * lax.rsqrt(jnp.mean(y * y, axis=-1, keepdims=True) + EPS) * nw_ref[:, gs]
            y_buf[slot, rows, gs] = y.astype(BF16)

        return [decays] + [functools.partial(group, g) for g in range(n_grp)]

    def run_block(slot, out_rows, fill_next):
        work = [functools.partial(conv_step, slot, j * conv_piece, (j + 1) * conv_piece) for j in range(3)]
        for c in range(tb // q_len):
            work += chunk_steps(c, slot)
        for j, step in enumerate(work):
            step()
            if j < len(fill_next):
                fill_next[j]()
        for step in fill_next[len(work):]:
            step()
        o_ref[0, out_rows, :] = h_ref[0, out_rows, :] + _dot(y_buf[slot], wout_ref[...])

    @pl.when(pl.program_id(1) == 0)
    def _():
        st_ref[...] = jnp.zeros_like(st_ref)
        tail_ref[...] = jnp.zeros_like(tail_ref)
        for step in projection(lambda: h_ref[0, 0:tb, :], 0):
            step()

    run_block(0, pl.ds(0, tb), projection(lambda: h_ref[0, tb:2 * tb, :], 1))
    run_block(1, pl.ds(tb, tb), projection(lambda: hn_ref[0], 0))


def _ssd_mixer(h, bsz, mix_norm, w_in, conv_w, conv_b, dt_bias, a_log, d_skip, norm_w, w_out, tb=256):
    t, d = h.shape
    s = t // bsz
    win = _pad_cols(w_in, SSD_D_INNER + SSD_CONV_DIM + LANES).astype(BF16)
    wout = w_out.astype(BF16)
    head_of_lane = jnp.arange(SSD_D_INNER) // SSD_HEAD_DIM
    expand = (jnp.arange(LANES)[:, None] == head_of_lane[None, :]).astype(BF16)

    def pad_row(v):
        return jnp.pad(v, (0, LANES - v.shape[0])).reshape(1, LANES)

    last_blk = s // tb - 1
    h3 = h.reshape(bsz, s, d)
    out = pl.pallas_call(
        functools.partial(_ssd_kernel, tb=tb),
        grid=(bsz, s // (2 * tb)),
        in_specs=[pl.BlockSpec((1, 2 * tb, d), lambda b, i: (b, i, 0)),
                  pl.BlockSpec((1, tb, d), lambda b, i: (b, jnp.minimum(2 * i + 2, last_blk), 0)),
                  _const_spec((1, d)), _const_spec(win.shape),
                  _const_spec((SSD_CONV, SSD_CONV_DIM)), _const_spec((1, SSD_CONV_DIM)),
                  _const_spec((1, LANES)), _const_spec((1, LANES)),
                  _const_spec((1, SSD_D_INNER)), _const_spec((1, SSD_D_INNER)),
                  _const_spec((LANES, SSD_D_INNER)), _const_spec(wout.shape)],
        out_specs=pl.BlockSpec((1, 2 * tb, d), lambda b, i: (b, i, 0)),
        out_shape=jax.ShapeDtypeStruct((bsz, s, d), F32),
        scratch_shapes=[pltpu.VMEM((SSD_GROUPS, SSD_STATE, SSD_HPG * SSD_HEAD_DIM), F32),
                        pltpu.VMEM((tb, d), BF16),
                        pltpu.VMEM((SSD_HALO, SSD_CONV_DIM), F32),
                        pltpu.VMEM((tb, SSD_CONV_DIM), F32),
                        pltpu.VMEM((2, tb, SSD_D_INNER), F32),
                        pltpu.VMEM((2, tb, SSD_CONV_DIM), F32),
                        pltpu.VMEM((2, tb, LANES), F32),
                        pltpu.VMEM((2, tb, SSD_D_INNER), BF16)],
        compiler_params=_params("parallel", "arbitrary"),
        name="ssd",
    )(h3, h3, mix_norm.reshape(1, d), win, conv_w, conv_b.reshape(1, -1), pad_row(dt_bias), pad_row(a_log),
      jnp.repeat(d_skip, SSD_HEAD_DIM).reshape(1, -1), norm_w.reshape(1, -1), expand, wout)
    return out.reshape(t, d)


def _mla_qkv_kernel(h_ref, pos_ref, nw_ref, win_ref, qn_ref, kvn_ref, wuq_ref, wuk_ref, wuv_ref, freq_ref,
                    q_ref, k_ref, v_ref):
    half = MLA_ROPE // 2
    o1, o2 = MLA_Q_LORA, MLA_Q_LORA + MLA_KV_LORA
    lat = _dot(_rms(h_ref[...], nw_ref[...]).astype(BF16), win_ref[...])
    cqn = _rms(lat[:, :o1], qn_ref[...]).astype(BF16)
    ckvn = _rms(lat[:, o1:o2], kvn_ref[...]).astype(BF16)
    tm = pos_ref.shape[0]
    lane = lax.broadcasted_iota(jnp.int32, (tm, LANES), 1)
    lane_h = lax.broadcasted_iota(jnp.int32, (tm // 2, LANES), 1)
    freq = freq_ref[...]
    ang2 = jnp.where(lane_h < MLA_ROPE, pos_ref[0:tm // 2, :] * freq, pos_ref[tm // 2:tm, :] * freq)
    cos2, sin2 = jnp.cos(ang2), jnp.sin(ang2)
    cos_a = jnp.concatenate([cos2, pltpu.roll(cos2, MLA_ROPE, 1)], axis=0)
    sin_a = jnp.concatenate([sin2, pltpu.roll(sin2, MLA_ROPE, 1)], axis=0)
    cos_v = jnp.where(lane < MLA_ROPE, cos_a, 0.0)
    sin_v = jnp.where(lane < half, -sin_a, jnp.where(lane < MLA_ROPE, sin_a, 0.0))

    def rope(t):
        return t * cos_v + pltpu.roll(t, MLA_ROPE, 1) * sin_v

    scale = MLA_QK ** -0.5 * math.log2(math.e)
    q = _dot(cqn, wuq_ref[...])
    k_rope = rope(lat[:, o2:o2 + LANES]).astype(BF16)
    k_nope = _dot(ckvn, wuk_ref[...])
    for h in range(MLA_HEADS):
        base = h * MLA_QK_PAD
        q_ref[:, base:base + MLA_NOPE] = (q[:, base:base + MLA_NOPE] * scale).astype(BF16)
        q_ref[:, base + MLA_NOPE:base + MLA_QK_PAD] = (
            rope(q[:, base + MLA_NOPE:base + MLA_QK_PAD]) * scale).astype(BF16)
        k_ref[:, base:base + MLA_NOPE] = k_nope[:, h * MLA_NOPE:(h + 1) * MLA_NOPE].astype(BF16)
        k_ref[:, base + MLA_NOPE:base + MLA_QK_PAD] = k_rope
    v = _dot(ckvn, wuv_ref[...])
    ones_col = jnp.where(lane == 0, 1.0, 0.0).astype(BF16)
    for h in range(MLA_HEADS):
        base = h * MLA_V_PAD
        v_ref[:, base:base + MLA_V] = v[:, h * MLA_V:(h + 1) * MLA_V].astype(BF16)
        v_ref[:, base + MLA_V:base + MLA_V_PAD] = ones_col


def _flash_kernel(q_ref, k_ref, v_ref, o_ref, acc_ref, *, tq, hps):
    qi = pl.program_id(2)
    acc_ref[...] = jnp.zeros_like(acc_ref)
    row = lax.broadcasted_iota(jnp.int32, (tq, tq), 0)
    col = lax.broadcasted_iota(jnp.int32, (tq, tq), 1)

    def step(blocks, ms, mask_last):
        key_rows = [pl.ds(pl.multiple_of(j * tq, tq), tq) for j in blocks]
        new_ms = []
        for hd in range(hps):
            q = q_ref[0, :, hd * MLA_QK_PAD:(hd + 1) * MLA_QK_PAD]
            scores = []
            for n, ks in enumerate(key_rows):
                k = k_ref[0, ks, hd * MLA_QK_PAD:(hd + 1) * MLA_QK_PAD]
                s = lax.dot_general(q, k, NT_DIMS, preferred_element_type=F32)
                if mask_last and n == len(key_rows) - 1:
                    s = jnp.where(col <= row, s, -jnp.inf)
                scores.append(s)
            m_new = ms[hd]
            for s in scores:
                m_new = jnp.maximum(m_new, jnp.max(s, axis=-1, keepdims=True))
            pv = None
            for s, ks in zip(scores, key_rows):
                p = jnp.exp2((s - m_new).astype(BF16))
                t = _dot(p, v_ref[0, ks, hd * MLA_V_PAD:(hd + 1) * MLA_V_PAD])
                pv = t if pv is None else pv + t
            acc_ref[hd] = jnp.exp2(ms[hd] - m_new) * acc_ref[hd] + pv
            new_ms.append(m_new)
        return tuple(new_ms)

    init = tuple(jnp.full((tq, 1), -jnp.inf, F32) for _ in range(hps))
    quads = qi // 4

    def two_pairs(j, cr):
        return step([4 * j + 2, 4 * j + 3], step([4 * j, 4 * j + 1], cr, False), False)

    ms = lax.fori_loop(0, quads, two_pairs, init)
    ms = lax.cond((qi // 2) % 2 == 1,
                  lambda cr: step([4 * quads, 4 * quads + 1], cr, False),
                  lambda cr: cr,
                  ms)
    lax.cond(qi % 2 == 1,
             lambda cr: step([qi - 1, qi], cr, True),
             lambda cr: step([qi], cr, True),
             ms)
    for hd in range(hps):
        acc = acc_ref[hd]
        o_ref[0, :, hd * MLA_V:(hd + 1) * MLA_V] = (acc[:, :MLA_V] / acc[:, MLA_V:MLA_V + 1]).astype(o_ref.dtype)


def _mla_mixer(h, bsz, positions, mix_norm, w_in, q_norm, w_uq, kv_norm, w_ukv, w_out, tm=512, tq=512):
    t, d = h.shape
    s = t // bsz
    half = MLA_ROPE // 2

    def swap_halves(w):
        return jnp.concatenate([w[..., half:], w[..., :half]], axis=-1)

    o1, o2 = MLA_Q_LORA, MLA_Q_LORA + MLA_KV_LORA
    win = jnp.concatenate([w_in, swap_halves(w_in[:, o2:])], axis=-1).astype(BF16)

    wq = w_uq.reshape(MLA_Q_LORA, MLA_HEADS, MLA_QK)
    wq_rope = wq[..., MLA_NOPE:]
    wuq = jnp.concatenate([wq[..., :MLA_NOPE], wq_rope, swap_halves(wq_rope)], axis=-1)
    wuq = wuq.reshape(MLA_Q_LORA, MLA_HEADS * MLA_QK_PAD).astype(BF16)
    wkv = w_ukv.reshape(MLA_KV_LORA, MLA_HEADS, MLA_NOPE + MLA_V)
    wuk = wkv[..., :MLA_NOPE].reshape(MLA_KV_LORA, MLA_HEADS * MLA_NOPE).astype(BF16)
    wuv = wkv[..., MLA_NOPE:].reshape(MLA_KV_LORA, MLA_HEADS * MLA_V).astype(BF16)
    inv_freq = 1.0 / (ROPE_THETA ** (jnp.arange(0, MLA_ROPE, 2, dtype=F32) / MLA_ROPE))
    freq = jnp.tile(inv_freq, LANES // half).reshape(1, LANES)
    pos = positions.astype(F32).reshape(t, 1)

    def rows(n):
        return pl.BlockSpec((tm, n), lambda i: (i, 0))

    qk_w = MLA_HEADS * MLA_QK_PAD
    v_w = MLA_HEADS * MLA_V
    vp_w = MLA_HEADS * MLA_V_PAD
    qf, kf, vf = pl.pallas_call(
        _mla_qkv_kernel,
        grid=(t // tm,),
        in_specs=[rows(d), rows(1), _const_spec((1, d)), _const_spec(win.shape),
                  _const_spec((1, MLA_Q_LORA)), _const_spec((1, MLA_KV_LORA)),
                  _const_spec(wuq.shape), _const_spec(wuk.shape), _const_spec(wuv.shape),
                  _const_spec((1, LANES))],
        out_specs=[rows(qk_w), rows(qk_w), rows(vp_w)],
        out_shape=[jax.ShapeDtypeStruct((t, qk_w), BF16), jax.ShapeDtypeStruct((t, qk_w), BF16),
                   jax.ShapeDtypeStruct((t, vp_w), BF16)],
        compiler_params=_params("parallel"),
        name="mla_qkv",
    )(h, pos, mix_norm.reshape(1, d), win, q_norm.reshape(1, -1), kv_norm.reshape(1, -1), wuq, wuk, wuv, freq)

    hps = FLASH_HEADS_PER_STEP
    o = pl.pallas_call(
        functools.partial(_flash_kernel, tq=tq, hps=hps),
        grid=(bsz, MLA_HEADS // hps, s // tq),
        in_specs=[pl.BlockSpec((1, tq, hps * MLA_QK_PAD), lambda b, hd, i: (b, i, hd)),
                  pl.BlockSpec((1, s, hps * MLA_QK_PAD), lambda b, hd, i: (b, 0, hd)),
                  pl.BlockSpec((1, s, hps * MLA_V_PAD), lambda b, hd, i: (b, 0, hd))],
        out_specs=pl.BlockSpec((1, tq, hps * MLA_V), lambda b, hd, i: (b, i, hd)),
        out_shape=jax.ShapeDtypeStruct((bsz, s, v_w), BF16),
        scratch_shapes=[pltpu.VMEM((hps, tq, MLA_V_PAD), F32)],
        compiler_params=_params("parallel", "parallel", "arbitrary"),
        name="mla_flash",
    )(qf.reshape(bsz, s, qk_w), kf.reshape(bsz, s, qk_w), vf.reshape(bsz, s, vp_w))
    return o.reshape(t, v_w), w_out.astype(BF16)


def kernel(x, positions, l0_ffn1_norm, l0_ffn1_w_gu, l0_ffn1_w_down, l0_mix_norm, l0_gla_w_in, l0_gla_w_gate_b, l0_gla_b_gate, l0_gla_norm, l0_gla_w_out, l0_ffn2_norm, l0_ffn2_w_gu, l0_ffn2_w_down, l1_ffn1_norm, l1_ffn1_w_gu, l1_ffn1_w_down, l1_mix_norm, l1_ssd_w_in, l1_ssd_conv_w, l1_ssd_conv_b, l1_ssd_dt_bias, l1_ssd_a_log, l1_ssd_d_skip, l1_ssd_norm, l1_ssd_w_out, l1_ffn2_norm, l1_ffn2_w_gu, l1_ffn2_w_down, l2_ffn1_norm, l2_ffn1_w_gu, l2_ffn1_w_down, l2_mix_norm, l2_mla_w_in, l2_mla_q_norm, l2_mla_w_uq, l2_mla_kv_norm, l2_mla_w_ukv, l2_mla_w_out, l2_ffn2_norm, l2_ffn2_w_gu, l2_ffn2_w_down, l3_ffn1_norm, l3_ffn1_w_gu, l3_ffn1_w_down, l3_mix_norm, l3_gla_w_in, l3_gla_w_gate_b, l3_gla_b_gate, l3_gla_norm, l3_gla_w_out, l3_ffn2_norm, l3_ffn2_w_gu, l3_ffn2_w_down, final_norm):
    bsz, s, d = x.shape
    h = x.reshape(bsz * s, d)

    h = _ffn(h, l0_ffn1_norm, l0_ffn1_w_gu, l0_ffn1_w_down)
    h = _gla_mixer(h, bsz, l0_mix_norm, l0_gla_w_in, l0_gla_w_gate_b, l0_gla_b_gate, l0_gla_norm, l0_gla_w_out)
    h = _ffn(h, l0_ffn2_norm, l0_ffn2_w_gu, l0_ffn2_w_down)

    h = _ffn(h, l1_ffn1_norm, l1_ffn1_w_gu, l1_ffn1_w_down)
    h = _ssd_mixer(h, bsz, l1_mix_norm, l1_ssd_w_in, l1_ssd_conv_w, l1_ssd_conv_b, l1_ssd_dt_bias,
                   l1_ssd_a_log, l1_ssd_d_skip, l1_ssd_norm, l1_ssd_w_out)
    h = _ffn(h, l1_ffn2_norm, l1_ffn2_w_gu, l1_ffn2_w_down)

    h = _ffn(h, l2_ffn1_norm, l2_ffn1_w_gu, l2_ffn1_w_down)
    attn = _mla_mixer(h, bsz, positions, l2_mix_norm, l2_mla_w_in, l2_mla_q_norm, l2_mla_w_uq,
                      l2_mla_kv_norm, l2_mla_w_ukv, l2_mla_w_out)
    h = _ffn(h, l2_ffn2_norm, l2_ffn2_w_gu, l2_ffn2_w_down, mixer_out=attn)

    h = _ffn(h, l3_ffn1_norm, l3_ffn1_w_gu, l3_ffn1_w_down)
    h = _gla_mixer(h, bsz, l3_mix_norm, l3_gla_w_in, l3_gla_w_gate_b, l3_gla_b_gate, l3_gla_norm, l3_gla_w_out)
    h = _ffn(h, l3_ffn2_norm, l3_ffn2_w_gu, l3_ffn2_w_down, final_w=final_norm)
    return h.reshape(bsz, s, d)
```

```python
import functools
import math

import jax
import jax.numpy as jnp
from jax import lax
from jax.experimental import pallas as pl
from jax.experimental.pallas import tpu as pltpu

F32 = jnp.float32
BF16 = jnp.bfloat16
EPS = 1e-6

V7X_VMEM_BYTES = 64 * 1024 * 1024
VMEM_LIMIT = V7X_VMEM_BYTES - 8 * 1024 * 1024
LANES = 128

D_FF = 2816
FF_CHUNK = 256

GLA_HEADS = 4
GLA_DK_H = 128
GLA_DV_H = 256
GLA_GATE_RANK = 16
GLA_GATE_TEMP = 16.0
GLA_CHUNK = 64

SSD_D_INNER = 2048
SSD_HEAD_DIM = 64
SSD_HEADS = 32
SSD_GROUPS = 4
SSD_HPG = 8
SSD_STATE = 128
SSD_CONV = 4
SSD_CHUNK = 128
SSD_CONV_DIM = 3072
SSD_HALO = 8

MLA_HEADS = 8
MLA_Q_LORA = 384
MLA_KV_LORA = 256
MLA_NOPE = 128
MLA_ROPE = 64
MLA_V = 128
MLA_QK = MLA_NOPE + MLA_ROPE
MLA_QK_PAD = 256
ROPE_THETA = 10000.0
MLA_V_PAD = 256
FLASH_HEADS_PER_STEP = 4

NT_DIMS = (((1,), (1,)), ((), ()))
TN_DIMS = (((0,), (0,)), ((), ()))


def _params(*sem):
    return pltpu.CompilerParams(dimension_semantics=sem, vmem_limit_bytes=VMEM_LIMIT)


def _const_spec(shape):
    return pl.BlockSpec(shape, lambda *_: (0,) * len(shape), pipeline_mode=pl.Buffered(1))


def _rms(x, w):
    return x * lax.rsqrt(jnp.mean(x * x, axis=-1, keepdims=True) + EPS) * w


def _silu(x):
    return x / (1.0 + jnp.exp(-x))


def _softplus(x):
    return jnp.maximum(x, 0.0) + jnp.log(1.0 + jnp.exp(-jnp.abs(x)))


def _dot(a, b):
    return jnp.dot(a, b, preferred_element_type=F32)


def _split_bf16(a, terms):
    parts = []
    for _ in range(terms):
        p = a.astype(BF16)
        parts.append(p)
        a = a - p.astype(F32)
    return parts


def _dot_exact_rhs(a, b, terms, dims=None):
    out = None
    for p in _split_bf16(a, terms):
        if dims is None:
            t = _dot(p, b)
        else:
            t = lax.dot_general(p, b, dims, preferred_element_type=F32)
        out = t if out is None else out + t
    return out


def _transpose_via_mxu(eye, a, terms=2):
    pieces = jnp.concatenate(_split_bf16(a, terms), axis=1)
    return lax.dot_general(jnp.concatenate([eye] * terms, axis=1), pieces, NT_DIMS, preferred_element_type=F32)


def _cumsum_via_mxu(tril, a, terms=2):
    pieces = jnp.concatenate(_split_bf16(a, terms), axis=0)
    return _dot(jnp.concatenate([tril] * terms, axis=1), pieces)


def _ffn_kernel(*refs, n_chunks, tf, final, mixer_out):
    refs = list(refs)
    x_ref = refs.pop(0)
    if mixer_out:
        a_ref, wo_ref = refs.pop(0), refs.pop(0)
    nw_ref, wgu_ref, wd_ref = refs.pop(0), refs.pop(0), refs.pop(0)
    if final:
        fw_ref = refs.pop(0)
    o_ref, xn_ref, acc_ref = refs[:3]
    if mixer_out:
        x_in = refs[3]
        x_in[...] = x_ref[...] + _dot(a_ref[...], wo_ref[...])
    else:
        x_in = x_ref
    x_ref = x_in
    d_ff = n_chunks * tf
    xn_ref[...] = _rms(x_ref[...], nw_ref[...]).astype(BF16)

    def down(c):
        if isinstance(c, int):
            g_cols, u_cols = pl.ds(c * tf, tf), pl.ds(d_ff + c * tf, tf)
        else:
            g_cols = pl.ds(pl.multiple_of(c * tf, tf), tf)
            u_cols = pl.ds(pl.multiple_of(d_ff + c * tf, tf), tf)
        xn = xn_ref[...]
        a = (_silu(_dot(xn, wgu_ref[:, g_cols])) * _dot(xn, wgu_ref[:, u_cols])).astype(BF16)
        return _dot(a, wd_ref[g_cols, :])

    acc_ref[...] = down(0)

    for c in range(1, n_chunks - 1):
        acc_ref[...] += down(c)
    y = x_ref[...] + (acc_ref[...] + down(n_chunks - 1))
    if final:
        y = _rms(y, fw_ref[...])
    o_ref[...] = y


def _ffn(h, norm_w, w_gu, w_down, final_w=None, mixer_out=None, tm=1024):
    t, d = h.shape
    n_chunks = D_FF // FF_CHUNK
    wgu = w_gu.astype(BF16)
    wd = (0.5 * w_down).astype(BF16)
    args = [h]
    in_specs = [pl.BlockSpec((tm, d), lambda i: (i, 0))]
    scratch = [pltpu.VMEM((tm, d), BF16), pltpu.VMEM((tm, d), F32)]
    if mixer_out is not None:
        a, w_out = mixer_out
        args += [a, w_out]
        in_specs += [pl.BlockSpec((tm, a.shape[1]), lambda i: (i, 0)), _const_spec(w_out.shape)]
        scratch.append(pltpu.VMEM((tm, d), F32))
    args += [norm_w.reshape(1, d), wgu, wd]
    in_specs += [_const_spec((1, d)), _const_spec(wgu.shape), _const_spec(wd.shape)]
    if final_w is not None:
        args.append(final_w.reshape(1, d))
        in_specs.append(_const_spec((1, d)))
    return pl.pallas_call(
        functools.partial(_ffn_kernel, n_chunks=n_chunks, tf=FF_CHUNK, final=final_w is not None,
                          mixer_out=mixer_out is not None),
        grid=(t // tm,),
        in_specs=in_specs,
        out_specs=pl.BlockSpec((tm, d), lambda i: (i, 0)),
        out_shape=jax.ShapeDtypeStruct((t, d), F32),
        scratch_shapes=scratch,
        compiler_params=_params("parallel"),
        name="ffn",
    )(*args)


def _pad_cols(w, n):
    return jnp.pad(w, ((0, 0), (0, n - w.shape[1])))


def _gla_kernel(h_ref, hn_ref, nw_ref, win_ref, wgb_ref, bg_ref, gnw_ref, wout_ref, o_ref,
                st_ref, xn_ref, og_buf, q_buf, k_buf, v_buf, g_buf, la_buf, *, tb):
    c_len, heads, dk, dv = GLA_CHUNK, GLA_HEADS, GLA_DK_H, GLA_DV_H
    dk_all, dv_all = heads * dk, heads * dv
    q0, k0, v0, g0, a0 = 0, dk_all, 2 * dk_all, 2 * dk_all + dv_all, 2 * dk_all + 2 * dv_all

    row = lax.broadcasted_iota(jnp.int32, (c_len, c_len), 0)
    col = lax.broadcasted_iota(jnp.int32, (c_len, c_len), 1)
    causal = col <= row
    tril = jnp.where(causal, 1.0, 0.0).astype(BF16)
    scale = dk ** -0.5

    def projection(load_rows, slot):
        def norm():
            xn_ref[...] = _rms(load_rows(), nw_ref[...]).astype(BF16)

        def proj_q():
            q_buf[slot] = _dot(xn_ref[...], win_ref[:, q0:k0])

        def proj_k():
            k_buf[slot] = _dot(xn_ref[...], win_ref[:, k0:v0])

        def proj_v():
            v_buf[slot] = _dot(xn_ref[...], win_ref[:, v0:g0]).astype(BF16)

        def proj_g():
            g_buf[slot] = _dot(xn_ref[...], win_ref[:, g0:a0])

        def gate():
            a_hi, a_lo = _split_bf16(_dot(xn_ref[...], win_ref[:, a0:a0 + LANES]), 2)
            z = _dot(a_hi, wgb_ref[0]) + (_dot(a_hi, wgb_ref[1]) + _dot(a_lo, wgb_ref[0])) + bg_ref[...]
            la_buf[slot] = (jnp.minimum(z, 0.0) - jnp.log(1.0 + jnp.exp(-jnp.abs(z)))) / GLA_GATE_TEMP

        return [norm, proj_q, proj_k, proj_v, proj_g, gate]

    def chunk(c, slot):
        rows = pl.ds(c * c_len, c_len)
        b = _cumsum_via_mxu(tril, la_buf[slot, rows, :])
        b_last = b[c_len - 1:c_len, :]
        q = q_buf[slot, rows, :]
        k = k_buf[slot, rows, :]
        q_dec = (q * scale * jnp.exp(b)).astype(BF16)
        k_inv = (k * jnp.exp(-b)).astype(BF16)
        k_end = (k * jnp.exp(b_last - b)).astype(BF16)
        decay = jnp.exp(b_last)
        v = v_buf[slot, rows, :]
        outs = []
        for h in range(heads):
            ks = slice(h * dk, (h + 1) * dk)
            vh = v[:, h * dv:(h + 1) * dv]
            qd = q_dec[:, ks]
            attn = lax.dot_general(qd, k_inv[:, ks], NT_DIMS, preferred_element_type=F32)
            attn = jnp.where(causal, attn, 0.0).astype(BF16)
            st = st_ref[h]
            o = _dot(attn, vh) + lax.dot_general(qd, st.astype(BF16), NT_DIMS,
                                                 preferred_element_type=F32)
            st_ref[h] = st * decay[:, ks] + lax.dot_general(vh, k_end[:, ks], TN_DIMS,
                                                            preferred_element_type=F32)
            outs.append(o * lax.rsqrt(jnp.mean(o * o, axis=-1, keepdims=True) + EPS))
        o = jnp.concatenate(outs, axis=-1)
        o = o * gnw_ref[...] * _silu(g_buf[slot, rows, :])
        og_buf[slot, rows, :] = o.astype(BF16)

    def run_block(slot, out_rows, fill_next):
        n_chunks = tb // c_len
        for c in range(n_chunks):
            chunk(c, slot)
            if c < len(fill_next):
                fill_next[c]()
        for step in fill_next[n_chunks:]:
            step()
        o_ref[0, out_rows, :] = h_ref[0, out_rows, :] + _dot(og_buf[slot], wout_ref[...])

    @pl.when(pl.program_id(1) == 0)
    def _():
        st_ref[...] = jnp.zeros_like(st_ref)
        for step in projection(lambda: h_ref[0, 0:tb, :], 0):
            step()

    run_block(0, pl.ds(0, tb), projection(lambda: h_ref[0, tb:2 * tb, :], 1))
    run_block(1, pl.ds(tb, tb), projection(lambda: hn_ref[0], 0))


def _gla_mixer(h, bsz, mix_norm, w_in, w_gate_b, b_gate, gn_w, w_out, tb=512):
    t, d = h.shape
    s = t // bsz
    dk_all, dv_all = GLA_HEADS * GLA_DK_H, GLA_HEADS * GLA_DV_H
    win = _pad_cols(w_in, 2 * dk_all + 2 * dv_all + LANES).astype(BF16)
    wgb = jnp.pad(w_gate_b, ((0, LANES - GLA_GATE_RANK), (0, 0)))
    wgb_hi = wgb.astype(BF16)
    wgb = jnp.stack([wgb_hi, (wgb - wgb_hi.astype(F32)).astype(BF16)])
    wout = w_out.astype(BF16)
    last_blk = s // tb - 1
    h3 = h.reshape(bsz, s, d)
    out = pl.pallas_call(
        functools.partial(_gla_kernel, tb=tb),
        grid=(bsz, s // (2 * tb)),
        in_specs=[pl.BlockSpec((1, 2 * tb, d), lambda b, i: (b, i, 0)),
                  pl.BlockSpec((1, tb, d), lambda b, i: (b, jnp.minimum(2 * i + 2, last_blk), 0)),
                  _const_spec((1, d)), _const_spec(win.shape), _const_spec(wgb.shape),
                  _const_spec((1, dk_all)), _const_spec((1, dv_all)), _const_spec(wout.shape)],
        out_specs=pl.BlockSpec((1, 2 * tb, d), lambda b, i: (b, i, 0)),
        out_shape=jax.ShapeDtypeStruct((bsz, s, d), F32),
        scratch_shapes=[pltpu.VMEM((GLA_HEADS, GLA_DV_H, GLA_DK_H), F32),
                        pltpu.VMEM((tb, d), BF16),
                        pltpu.VMEM((2, tb, dv_all), BF16),
                        pltpu.VMEM((2, tb, dk_all), F32), pltpu.VMEM((2, tb, dk_all), F32),
                        pltpu.VMEM((2, tb, dv_all), BF16), pltpu.VMEM((2, tb, dv_all), F32),
                        pltpu.VMEM((2, tb, dk_all), F32)],
        compiler_params=_params("parallel", "arbitrary"),
        name="gla",
    )(h3, h3, mix_norm.reshape(1, d), win, wgb, b_gate.reshape(1, dk_all), gn_w.reshape(1, dv_all), wout)
    return out.reshape(t, d)


def _ssd_kernel(h_ref, hn_ref, mnw_ref, win_ref, cw_ref, cb_ref, dtb_ref, alog_ref, dsk_ref, nw_ref, e_ref,
                wout_ref, o_ref, st_ref, xn_ref, tail_ref, conv_ref, z_buf, xbc_buf, dt_buf, y_buf, *, tb):
    q_len, n_grp, hpg, p_dim, n_st = SSD_CHUNK, SSD_GROUPS, SSD_HPG, SSD_HEAD_DIM, SSD_STATE
    gw = hpg * p_dim
    z0, x0, d0 = 0, SSD_D_INNER, SSD_D_INNER + SSD_CONV_DIM
    conv_piece = SSD_CONV_DIM // 3

    def projection(load_rows, slot):
        def norm():
            xn_ref[...] = _rms(load_rows(), mnw_ref[...]).astype(BF16)

        def proj_z(lo, hi):
            z_buf[slot, :, lo:hi] = _dot(xn_ref[...], win_ref[:, z0 + lo:z0 + hi])

        def proj_x(lo, hi):
            xbc_buf[slot, :, lo:hi] = _dot(xn_ref[...], win_ref[:, x0 + lo:x0 + hi])

        def proj_dt():
            dt_buf[slot] = _dot(xn_ref[...], win_ref[:, d0:d0 + LANES])

        half = SSD_D_INNER // 2
        steps = [norm, functools.partial(proj_z, 0, half), functools.partial(proj_z, half, SSD_D_INNER)]
        steps += [functools.partial(proj_x, j * conv_piece, (j + 1) * conv_piece) for j in range(3)]
        return steps + [proj_dt]

    def conv_step(slot, lo, hi):
        u = jnp.concatenate([tail_ref[:, lo:hi], xbc_buf[slot, :, lo:hi]], axis=0)
        tiles = u.reshape((tb + SSD_HALO) // SSD_HALO, SSD_HALO, hi - lo)
        sub = lax.broadcasted_iota(jnp.int32, (tb // SSD_HALO, SSD_HALO, hi - lo), 1)
        acc = cb_ref[:, lo:hi] + cw_ref[SSD_CONV - 1:SSD_CONV, lo:hi] * tiles[1:]
        for delay in range(1, SSD_CONV):
            rot = pltpu.roll(tiles, delay, 1)
            acc = acc + cw_ref[SSD_CONV - 1 - delay:SSD_CONV - delay, lo:hi] * jnp.where(
                sub >= delay, rot[1:], rot[:-1])
        conv_ref[:, lo:hi] = _silu(acc).reshape(tb, hi - lo)
        tail_ref[:, lo:hi] = xbc_buf[slot, tb - SSD_HALO:tb, lo:hi]

    row = lax.broadcasted_iota(jnp.int32, (q_len, q_len), 0)
    col = lax.broadcasted_iota(jnp.int32, (q_len, q_len), 1)
    causal = col <= row
    tril = jnp.where(causal, 1.0, 0.0).astype(BF16)
    er = lax.broadcasted_iota(jnp.int32, (LANES, LANES), 0)
    ec = lax.broadcasted_iota(jnp.int32, (LANES, LANES), 1)
    eye = jnp.where(er == ec, 1.0, 0.0).astype(BF16)
    lane = lax.broadcasted_iota(jnp.int32, (q_len, LANES), 1)
    low_half = lane < p_dim
    a_neg = -jnp.exp(alog_ref[...])

    def chunk_steps(c, slot):
        rows = pl.ds(c * q_len, q_len)
        shared = {}

        def decays():
            dt = _softplus(dt_buf[slot, rows, :] + dtb_ref[...])
            da = dt * a_neg
            a_cs = _cumsum_via_mxu(tril, da)
            a_last = a_cs[q_len - 1:q_len, :]
            a_cs_t = _transpose_via_mxu(eye, a_cs)
            shared["shift_t"] = a_cs_t - jnp.log(_transpose_via_mxu(eye, dt))
            e_mat = e_ref[...]
            shared["a_cs"] = a_cs
            both = jnp.concatenate([jnp.exp(a_cs), jnp.exp(a_last - a_cs) * dt], axis=0).astype(BF16)
            both_x = _dot(both, e_mat)
            shared["ea_x"] = both_x[:q_len]
            shared["w_x"] = both_x[q_len:]
            shared["tot_x"] = _dot_exact_rhs(jnp.exp(a_last), e_mat, 2)

        def group(g):
            a_cs, shift_t, ea_x, w_x = shared["a_cs"], shared["shift_t"], shared["ea_x"], shared["w_x"]
            tot_x = shared["tot_x"]
            gs = slice(g * gw, (g + 1) * gw)
            x_g = conv_ref[rows, gs]
            b_g = conv_ref[rows, SSD_D_INNER + g * n_st:SSD_D_INNER + (g + 1) * n_st].astype(BF16)
            c_g = conv_ref[rows, SSD_D_INNER + (n_grp + g) * n_st:
                           SSD_D_INNER + (n_grp + g + 1) * n_st].astype(BF16)
            cb = lax.dot_general(c_g, b_g, NT_DIMS, preferred_element_type=F32)
            st = st_ref[g]
            y_off = _dot(c_g, st.astype(BF16)) * ea_x[:, gs]
            xw = (x_g * w_x[:, gs]).astype(BF16)
            st_ref[g] = st * tot_x[:, gs] + lax.dot_general(b_g, xw, TN_DIMS, preferred_element_type=F32)
            xb = x_g.astype(BF16)
            ys = []
            for pr in range(hpg // 2):
                h1 = g * hpg + 2 * pr
                x_pair = xb[:, pr * LANES:(pr + 1) * LANES]
                mats, xs = [], []
                for hh, keep in ((h1, low_half), (h1 + 1, jnp.logical_not(low_half))):
                    seg = a_cs[:, hh:hh + 1] - shift_t[hh:hh + 1, :]
                    mats.append((cb * jnp.where(causal, jnp.exp(seg), 0.0)).astype(BF16))
                    xs.append(jnp.where(keep, x_pair, jnp.zeros_like(x_pair)))
                ys.append(_dot(jnp.concatenate(mats, axis=1), jnp.concatenate(xs, axis=0)))
            y = jnp.concatenate(ys, axis=-1) + y_off + x_g * dsk_ref[:, gs]
            y = y * _silu(z_buf[slot, rows, gs])
            y = y * lax.rsqrt(jnp.mean(y * y, axis=-1, keepdims=True) + EPS) * nw_ref[:, gs]
            y_buf[slot, rows, gs] = y.astype(BF16)

        return [decays] + [functools.partial(group, g) for g in range(n_grp)]

    def run_block(slot, out_rows, fill_next):
        work = [functools.partial(conv_step, slot, j * conv_piece, (j + 1) * conv_piece) for j in range(3)]
        for c in range(tb // q_len):
            work += chunk_steps(c, slot)
        for j, step in enumerate(work):
            step()
            if j < len(fill_next):
                fill_next[j]()
        for step in fill_next[len(work):]:
            step()
        o_ref[0, out_rows, :] = h_ref[0, out_rows, :] + _dot(y_buf[slot], wout_ref[...])

    @pl.when(pl.program_id(1) == 0)
    def _():
        st_ref[...] = jnp.zeros_like(st_ref)
        tail_ref[...] = jnp.zeros_like(tail_ref)
        for step in projection(lambda: h_ref[0, 0:tb, :], 0):
            step()

    run_block(0, pl.ds(0, tb), projection(lambda: h_ref[0, tb:2 * tb, :], 1))
    run_block(1, pl.ds(tb, tb), projection(lambda: hn_ref[0], 0))


def _ssd_mixer(h, bsz, mix_norm, w_in, conv_w, conv_b, dt_bias, a_log, d_skip, norm_w, w_out, tb=256):
    t, d = h.shape
    s = t // bsz
    win = _pad_cols(w_in, SSD_D_INNER + SSD_CONV_DIM + LANES).astype(BF16)
    wout = w_out.astype(BF16)
    head_of_lane = jnp.arange(SSD_D_INNER) // SSD_HEAD_DIM
    expand = (jnp.arange(LANES)[:, None] == head_of_lane[None, :]).astype(BF16)

    def pad_row(v):
        return jnp.pad(v, (0, LANES - v.shape[0])).reshape(1, LANES)

    last_blk = s // tb - 1
    h3 = h.reshape(bsz, s, d)
    out = pl.pallas_call(
        functools.partial(_ssd_kernel, tb=tb),
        grid=(bsz, s // (2 * tb)),
        in_specs=[pl.BlockSpec((1, 2 * tb, d), lambda b, i: (b, i, 0)),
                  pl.BlockSpec((1, tb, d), lambda b, i: (b, jnp.minimum(2 * i + 2, last_blk), 0)),
                  _const_spec((1, d)), _const_spec(win.shape),
                  _const_spec((SSD_CONV, SSD_CONV_DIM)), _const_spec((1, SSD_CONV_DIM)),
                  _const_spec((1, LANES)), _const_spec((1, LANES)),
                  _const_spec((1, SSD_D_INNER)), _const_spec((1, SSD_D_INNER)),
                  _const_spec((LANES, SSD_D_INNER)), _const_spec(wout.shape)],
        out_specs=pl.BlockSpec((1, 2 * tb, d), lambda b, i: (b, i, 0)),
        out_shape=jax.ShapeDtypeStruct((bsz, s, d), F32),
        scratch_shapes=[pltpu.VMEM((SSD_GROUPS, SSD_STATE, SSD_HPG * SSD_HEAD_DIM), F32),
                        pltpu.VMEM((tb, d), BF16),
                        pltpu.VMEM((SSD_HALO, SSD_CONV_DIM), F32),
                        pltpu.VMEM((tb, SSD_CONV_DIM), F32),
                        pltpu.VMEM((2, tb, SSD_D_INNER), F32),
                        pltpu.VMEM((2, tb, SSD_CONV_DIM), F32),
                        pltpu.VMEM((2, tb, LANES), F32),
                        pltpu.VMEM((2, tb, SSD_D_INNER), BF16)],
        compiler_params=_params("parallel", "arbitrary"),
        name="ssd",
    )(h3, h3, mix_norm.reshape(1, d), win, conv_w, conv_b.reshape(1, -1), pad_row(dt_bias), pad_row(a_log),
      jnp.repeat(d_skip, SSD_HEAD_DIM).reshape(1, -1), norm_w.reshape(1, -1), expand, wout)
    return out.reshape(t, d)


def _mla_qkv_kernel(h_ref, pos_ref, nw_ref, win_ref, qn_ref, kvn_ref, wuq_ref, wuk_ref, wuv_ref, freq_ref,
                    q_ref, k_ref, v_ref):
    half = MLA_ROPE // 2
    o1, o2 = MLA_Q_LORA, MLA_Q_LORA + MLA_KV_LORA
    lat = _dot(_rms(h_ref[...], nw_ref[...]).astype(BF16), win_ref[...])
    cqn = _rms(lat[:, :o1], qn_ref[...]).astype(BF16)
    ckvn = _rms(lat[:, o1:o2], kvn_ref[...]).astype(BF16)
    tm = pos_ref.shape[0]
    lane = lax.broadcasted_iota(jnp.int32, (tm, LANES), 1)
    lane_h = lax.broadcasted_iota(jnp.int32, (tm // 2, LANES), 1)
    freq = freq_ref[...]
    ang2 = jnp.where(lane_h < MLA_ROPE, pos_ref[0:tm // 2, :] * freq, pos_ref[tm // 2:tm, :] * freq)
    cos2, sin2 = jnp.cos(ang2), jnp.sin(ang2)
    cos_a = jnp.concatenate([cos2, pltpu.roll(cos2, MLA_ROPE, 1)], axis=0)
    sin_a = jnp.concatenate([sin2, pltpu.roll(sin2, MLA_ROPE, 1)], axis=0)
    cos_v = jnp.where(lane < MLA_ROPE, cos_a, 0.0)
    sin_v = jnp.where(lane < half, -sin_a, jnp.where(lane < MLA_ROPE, sin_a, 0.0))

    def rope(t):
        return t * cos_v + pltpu.roll(t, MLA_ROPE, 1) * sin_v

    scale = MLA_QK ** -0.5 * math.log2(math.e)
    q = _dot(cqn, wuq_ref[...])
    k_rope = rope(lat[:, o2:o2 + LANES]).astype(BF16)
    k_nope = _dot(ckvn, wuk_ref[...])
    for h in range(MLA_HEADS):
        base = h * MLA_QK_PAD
        q_ref[:, base:base + MLA_NOPE] = (q[:, base:base + MLA_NOPE] * scale).astype(BF16)
        q_ref[:, base + MLA_NOPE:base + MLA_QK_PAD] = (
            rope(q[:, base + MLA_NOPE:base + MLA_QK_PAD]) * scale).astype(BF16)
        k_ref[:, base:base + MLA_NOPE] = k_nope[:, h * MLA_NOPE:(h + 1) * MLA_NOPE].astype(BF16)
        k_ref[:, base + MLA_NOPE:base + MLA_QK_PAD] = k_rope
    v = _dot(ckvn, wuv_ref[...])
    ones_col = jnp.where(lane == 0, 1.0, 0.0).astype(BF16)
    for h in range(MLA_HEADS):
        base = h * MLA_V_PAD
        v_ref[:, base:base + MLA_V] = v[:, h * MLA_V:(h + 1) * MLA_V].astype(BF16)
        v_ref[:, base + MLA_V:base + MLA_V_PAD] = ones_col


def _flash_kernel(q_ref, k_ref, v_ref, o_ref, acc_ref, *, tq, hps):
    qi = pl.program_id(2)
    acc_ref[...] = jnp.zeros_like(acc_ref)
    row = lax.broadcasted_iota(jnp.int32, (tq, tq), 0)
    col = lax.broadcasted_iota(jnp.int32, (tq, tq), 1)

    def step(blocks, ms, mask_last):
        key_rows = [pl.ds(pl.multiple_of(j * tq, tq), tq) for j in blocks]
        new_ms = []
        for hd in range(hps):
            q = q_ref[0, :, hd * MLA_QK_PAD:(hd + 1) * MLA_QK_PAD]
            scores = []
            for n, ks in enumerate(key_rows):
                k = k_ref[0, ks, hd * MLA_QK_PAD:(hd + 1) * MLA_QK_PAD]
                s = lax.dot_general(q, k, NT_DIMS, preferred_element_type=F32)
                if mask_last and n == len(key_rows) - 1:
                    s = jnp.where(col <= row, s, -jnp.inf)
                scores.append(s)
            m_new = ms[hd]
            for s in scores:
                m_new = jnp.maximum(m_new, jnp.max(s, axis=-1, keepdims=True))
            pv = None
            for s, ks in zip(scores, key_rows):
                p = jnp.exp2((s - m_new).astype(BF16))
                t = _dot(p, v_ref[0, ks, hd * MLA_V_PAD:(hd + 1) * MLA_V_PAD])
                pv = t if pv is None else pv + t
            acc_ref[hd] = jnp.exp2(ms[hd] - m_new) * acc_ref[hd] + pv
            new_ms.append(m_new)
        return tuple(new_ms)

    init = tuple(jnp.full((tq, 1), -jnp.inf, F32) for _ in range(hps))
    quads = qi // 4

    def two_pairs(j, cr):
        return step([4 * j + 2, 4 * j + 3], step([4 * j, 4 * j + 1], cr, False), False)

    ms = lax.fori_loop(0, quads, two_pairs, init)
    ms = lax.cond((qi // 2) % 2 == 1,
                  lambda cr: step([4 * quads, 4 * quads + 1], cr, False),
                  lambda cr: cr,
                  ms)
    lax.cond(qi % 2 == 1,
             lambda cr: step([qi - 1, qi], cr, True),
             lambda cr: step([qi], cr, True),
             ms)
    for hd in range(hps):
        acc = acc_ref[hd]
        o_ref[0, :, hd * MLA_V:(hd + 1) * MLA_V] = (acc[:, :MLA_V] / acc[:, MLA_V:MLA_V + 1]).astype(o_ref.dtype)


def _mla_mixer(h, bsz, positions, mix_norm, w_in, q_norm, w_uq, kv_norm, w_ukv, w_out, tm=512, tq=512):
    t, d = h.shape
    s = t // bsz
    half = MLA_ROPE // 2

    def swap_halves(w):
        return jnp.concatenate([w[..., half:], w[..., :half]], axis=-1)

    o1, o2 = MLA_Q_LORA, MLA_Q_LORA + MLA_KV_LORA
    win = jnp.concatenate([w_in, swap_halves(w_in[:, o2:])], axis=-1).astype(BF16)

    wq = w_uq.reshape(MLA_Q_LORA, MLA_HEADS, MLA_QK)
    wq_rope = wq[..., MLA_NOPE:]
    wuq = jnp.concatenate([wq[..., :MLA_NOPE], wq_rope, swap_halves(wq_rope)], axis=-1)
    wuq = wuq.reshape(MLA_Q_LORA, MLA_HEADS * MLA_QK_PAD).astype(BF16)
    wkv = w_ukv.reshape(MLA_KV_LORA, MLA_HEADS, MLA_NOPE + MLA_V)
    wuk = wkv[..., :MLA_NOPE].reshape(MLA_KV_LORA, MLA_HEADS * MLA_NOPE).astype(BF16)
    wuv = wkv[..., MLA_NOPE:].reshape(MLA_KV_LORA, MLA_HEADS * MLA_V).astype(BF16)
    inv_freq = 1.0 / (ROPE_THETA ** (jnp.arange(0, MLA_ROPE, 2, dtype=F32) / MLA_ROPE))
    freq = jnp.tile(inv_freq, LANES // half).reshape(1, LANES)
    pos = positions.astype(F32).reshape(t, 1)

    def rows(n):
        return pl.BlockSpec((tm, n), lambda i: (i, 0))

    qk_w = MLA_HEADS * MLA_QK_PAD
    v_w = MLA_HEADS * MLA_V
    vp_w = MLA_HEADS * MLA_V_PAD
    qf, kf, vf = pl.pallas_call(
        _mla_qkv_kernel,
        grid=(t // tm,),
        in_specs=[rows(d), rows(1), _const_spec((1, d)), _const_spec(win.shape),
                  _const_spec((1, MLA_Q_LORA)), _const_spec((1, MLA_KV_LORA)),
                  _const_spec(wuq.shape), _const_spec(wuk.shape), _const_spec(wuv.shape),
                  _const_spec((1, LANES))],
        out_specs=[rows(qk_w), rows(qk_w), rows(vp_w)],
        out_shape=[jax.ShapeDtypeStruct((t, qk_w), BF16), jax.ShapeDtypeStruct((t, qk_w), BF16),
                   jax.ShapeDtypeStruct((t, vp_w), BF16)],
        compiler_params=_params("parallel"),
        name="mla_qkv",
    )(h, pos, mix_norm.reshape(1, d), win, q_norm.reshape(1, -1), kv_norm.reshape(1, -1), wuq, wuk, wuv, freq)

    hps = FLASH_HEADS_PER_STEP
    o = pl.pallas_call(
        functools.partial(_flash_kernel, tq=tq, hps=hps),
        grid=(bsz, MLA_HEADS // hps, s // tq),
        in_specs=[pl.BlockSpec((1, tq, hps * MLA_QK_PAD), lambda b, hd, i: (b, i, hd)),
                  pl.BlockSpec((1, s, hps * MLA_QK_PAD), lambda b, hd, i: (b, 0, hd)),
                  pl.BlockSpec((1, s, hps * MLA_V_PAD), lambda b, hd, i: (b, 0, hd))],
        out_specs=pl.BlockSpec((1, tq, hps * MLA_V), lambda b, hd, i: (b, i, hd)),
        out_shape=jax.ShapeDtypeStruct((bsz, s, v_w), BF16),
        scratch_shapes=[pltpu.VMEM((hps, tq, MLA_V_PAD), F32)],
        compiler_params=_params("parallel", "parallel", "arbitrary"),
        name="mla_flash",
    )(qf.reshape(bsz, s, qk_w), kf.reshape(bsz, s, qk_w), vf.reshape(bsz, s, vp_w))
    return o.reshape(t, v_w), w_out.astype(BF16)


def kernel(x, positions, l0_ffn1_norm, l0_ffn1_w_gu, l0_ffn1_w_down, l0_mix_norm, l0_gla_w_in, l0_gla_w_gate_b, l0_gla_b_gate, l0_gla_norm, l0_gla_w_out, l0_ffn2_norm, l0_ffn2_w_gu, l0_ffn2_w_down, l1_ffn1_norm, l1_ffn1_w_gu, l1_ffn1_w_down, l1_mix_norm, l1_ssd_w_in, l1_ssd_conv_w, l1_ssd_conv_b, l1_ssd_dt_bias, l1_ssd_a_log, l1_ssd_d_skip, l1_ssd_norm, l1_ssd_w_out, l1_ffn2_norm, l1_ffn2_w_gu, l1_ffn2_w_down, l2_ffn1_norm, l2_ffn1_w_gu, l2_ffn1_w_down, l2_mix_norm, l2_mla_w_in, l2_mla_q_norm, l2_mla_w_uq, l2_mla_kv_norm, l2_mla_w_ukv, l2_mla_w_out, l2_ffn2_norm, l2_ffn2_w_gu, l2_ffn2_w_down, l3_ffn1_norm, l3_ffn1_w_gu, l3_ffn1_w_down, l3_mix_norm, l3_gla_w_in, l3_gla_w_gate_b, l3_gla_b_gate, l3_gla_norm, l3_gla_w_out, l3_ffn2_norm, l3_ffn2_w_gu, l3_ffn2_w_down, final_norm):
    bsz, s, d = x.shape
    h = x.reshape(bsz * s, d)

    h = _ffn(h, l0_ffn1_norm, l0_ffn1_w_gu, l0_ffn1_w_down)
    h = _gla_mixer(h, bsz, l0_mix_norm, l0_gla_w_in, l0_gla_w_gate_b, l0_gla_b_gate, l0_gla_norm, l0_gla_w_out)
    h = _ffn(h, l0_ffn2_norm, l0_ffn2_w_gu, l0_ffn2_w_down)

    h = _ffn(h, l1_ffn1_norm, l1_ffn1_w_gu, l1_ffn1_w_down)
    h = _ssd_mixer(h, bsz, l1_mix_norm, l1_ssd_w_in, l1_ssd_conv_w, l1_ssd_conv_b, l1_ssd_dt_bias,
                   l1_ssd_a_log, l1_ssd_d_skip, l1_ssd_norm, l1_ssd_w_out)
    h = _ffn(h, l1_ffn2_norm, l1_ffn2_w_gu, l1_ffn2_w_down)

    h = _ffn(h, l2_ffn1_norm, l2_ffn1_w_gu, l2_ffn1_w_down)
    attn = _mla_mixer(h, bsz, positions, l2_mix_norm, l2_mla_w_in, l2_mla_q_norm, l2_mla_w_uq,
                      l2_mla_kv_norm, l2_mla_w_ukv, l2_mla_w_out)
    h = _ffn(h, l2_ffn2_norm, l2_ffn2_w_gu, l2_ffn2_w_down, mixer_out=attn)

    h = _ffn(h, l3_ffn1_norm, l3_ffn1_w_gu, l3_ffn1_w_down)
    h = _gla_mixer(h, bsz, l3_mix_norm, l3_gla_w_in, l3_gla_w_gate_b, l3_gla_b_gate, l3_gla_norm, l3_gla_w_out)
    h = _ffn(h, l3_ffn2_norm, l3_ffn2_w_gu, l3_ffn2_w_down, final_w=final_norm)
    return h.reshape(bsz, s, d)
```

```python
import functools
import math

import jax
import jax.numpy as jnp
from jax import lax
from jax.experimental import pallas as pl
from jax.experimental.pallas import tpu as pltpu

F32 = jnp.float32
BF16 = jnp.bfloat16
EPS = 1e-6

V7X_VMEM_BYTES = 64 * 1024 * 1024
VMEM_LIMIT = V7X_VMEM_BYTES - 8 * 1024 * 1024
LANES = 128

D_FF = 2816
FF_CHUNK = 256

GLA_HEADS = 4
GLA_DK_H = 128
GLA_DV_H = 256
GLA_GATE_RANK = 16
GLA_GATE_TEMP = 16.0
GLA_CHUNK = 64

SSD_D_INNER = 2048
SSD_HEAD_DIM = 64
SSD_HEADS = 32
SSD_GROUPS = 4
SSD_HPG = 8
SSD_STATE = 128
SSD_CONV = 4
SSD_CHUNK = 128
SSD_CONV_DIM = 3072
SSD_HALO = 8

MLA_HEADS = 8
MLA_Q_LORA = 384
MLA_KV_LORA = 256
MLA_NOPE = 128
MLA_ROPE = 64
MLA_V = 128
MLA_QK = MLA_NOPE + MLA_ROPE
MLA_QK_PAD = 256
ROPE_THETA = 10000.0
MLA_V_PAD = 256
FLASH_HEADS_PER_STEP = 4

NT_DIMS = (((1,), (1,)), ((), ()))
TN_DIMS = (((0,), (0,)), ((), ()))


def _params(*sem):
    return pltpu.CompilerParams(dimension_semantics=sem, vmem_limit_bytes=VMEM_LIMIT)


def _const_spec(shape):
    return pl.BlockSpec(shape, lambda *_: (0,) * len(shape), pipeline_mode=pl.Buffered(1))


def _rms(x, w):
    return x * lax.rsqrt(jnp.mean(x * x, axis=-1, keepdims=True) + EPS) * w


def _silu(x):
    return x / (1.0 + jnp.exp(-x))


def _softplus(x):
    return jnp.maximum(x, 0.0) + jnp.log(1.0 + jnp.exp(-jnp.abs(x)))


def _dot(a, b):
    return jnp.dot(a, b, preferred_element_type=F32)


def _split_bf16(a, terms):
    parts = []
    for _ in range(terms):
        p = a.astype(BF16)
        parts.append(p)
        a = a - p.astype(F32)
    return parts


def _dot_exact_rhs(a, b, terms, dims=None):
    out = None
    for p in _split_bf16(a, terms):
        if dims is None:
            t = _dot(p, b)
        else:
            t = lax.dot_general(p, b, dims, preferred_element_type=F32)
        out = t if out is None else out + t
    return out


def _transpose_via_mxu(eye, a, terms=2):
    pieces = jnp.concatenate(_split_bf16(a, terms), axis=1)
    return lax.dot_general(jnp.concatenate([eye] * terms, axis=1), pieces, NT_DIMS, preferred_element_type=F32)


def _cumsum_via_mxu(tril, a, terms=2):
    pieces = jnp.concatenate(_split_bf16(a, terms), axis=0)
    return _dot(jnp.concatenate([tril] * terms, axis=1), pieces)


def _ffn_kernel(*refs, n_chunks, tf, final, mixer_out):
    refs = list(refs)
    x_ref = refs.pop(0)
    if mixer_out:
        a_ref, wo_ref = refs.pop(0), refs.pop(0)
    nw_ref, wgu_ref, wd_ref = refs.pop(0), refs.pop(0), refs.pop(0)
    if final:
        fw_ref = refs.pop(0)
    o_ref, xn_ref, acc_ref = refs[:3]
    if mixer_out:
        x_in = refs[3]
        x_in[...] = x_ref[...] + _dot(a_ref[...], wo_ref[...])
    else:
        x_in = x_ref
    x_ref = x_in
    d_ff = n_chunks * tf
    xn_ref[...] = _rms(x_ref[...], nw_ref[...]).astype(BF16)

    def down(c):
        if isinstance(c, int):
            g_cols, u_cols = pl.ds(c * tf, tf), pl.ds(d_ff + c * tf, tf)
        else:
            g_cols = pl.ds(pl.multiple_of(c * tf, tf), tf)
            u_cols = pl.ds(pl.multiple_of(d_ff + c * tf, tf), tf)
        xn = xn_ref[...]
        a = (_silu(_dot(xn, wgu_ref[:, g_cols])) * _dot(xn, wgu_ref[:, u_cols])).astype(BF16)
        return _dot(a, wd_ref[g_cols, :])

    acc_ref[...] = down(0)

    for c in range(1, n_chunks - 1):
        acc_ref[...] += down(c)
    y = x_ref[...] + (acc_ref[...] + down(n_chunks - 1))
    if final:
        y = _rms(y, fw_ref[...])
    o_ref[...] = y


def _ffn(h, norm_w, w_gu, w_down, final_w=None, mixer_out=None, tm=1024):
    t, d = h.shape
    n_chunks = D_FF // FF_CHUNK
    wgu = w_gu.astype(BF16)
    wd = (0.5 * w_down).astype(BF16)
    args = [h]
    in_specs = [pl.BlockSpec((tm, d), lambda i: (i, 0))]
    scratch = [pltpu.VMEM((tm, d), BF16), pltpu.VMEM((tm, d), F32)]
    if mixer_out is not None:
        a, w_out = mixer_out
        args += [a, w_out]
        in_specs += [pl.BlockSpec((tm, a.shape[1]), lambda i: (i, 0)), _const_spec(w_out.shape)]
        scratch.append(pltpu.VMEM((tm, d), F32))
    args += [norm_w.reshape(1, d), wgu, wd]
    in_specs += [_const_spec((1, d)), _const_spec(wgu.shape), _const_spec(wd.shape)]
    if final_w is not None:
        args.append(final_w.reshape(1, d))
        in_specs.append(_const_spec((1, d)))
    return pl.pallas_call(
        functools.partial(_ffn_kernel, n_chunks=n_chunks, tf=FF_CHUNK, final=final_w is not None,
                          mixer_out=mixer_out is not None),
        grid=(t // tm,),
        in_specs=in_specs,
        out_specs=pl.BlockSpec((tm, d), lambda i: (i, 0)),
        out_shape=jax.ShapeDtypeStruct((t, d), F32),
        scratch_shapes=scratch,
        compiler_params=_params("parallel"),
        name="ffn",
    )(*args)


def _pad_cols(w, n):
    return jnp.pad(w, ((0, 0), (0, n - w.shape[1])))


def _gla_kernel(h_ref, hn_ref, nw_ref, win_ref, wgb_ref, bg_ref, gnw_ref, wout_ref, o_ref,
                st_ref, xn_ref, og_buf, q_buf, k_buf, v_buf, g_buf, la_buf, *, tb):
    c_len, heads, dk, dv = GLA_CHUNK, GLA_HEADS, GLA_DK_H, GLA_DV_H
    dk_all, dv_all = heads * dk, heads * dv
    q0, k0, v0, g0, a0 = 0, dk_all, 2 * dk_all, 2 * dk_all + dv_all, 2 * dk_all + 2 * dv_all

    row = lax.broadcasted_iota(jnp.int32, (c_len, c_len), 0)
    col = lax.broadcasted_iota(jnp.int32, (c_len, c_len), 1)
    causal = col <= row
    tril = jnp.where(causal, 1.0, 0.0).astype(BF16)
    scale = dk ** -0.5

    def projection(load_rows, slot):
        def norm():
            xn_ref[...] = _rms(load_rows(), nw_ref[...]).astype(BF16)

        def proj_q():
            q_buf[slot] = _dot(xn_ref[...], win_ref[:, q0:k0])

        def proj_k():
            k_buf[slot] = _dot(xn_ref[...], win_ref[:, k0:v0])

        def proj_v():
            v_buf[slot] = _dot(xn_ref[...], win_ref[:, v0:g0]).astype(BF16)

        def proj_g():
            g_buf[slot] = _dot(xn_ref[...], win_ref[:, g0:a0])

        def gate():
            a_hi, a_lo = _split_bf16(_dot(xn_ref[...], win_ref[:, a0:a0 + LANES]), 2)
            z = _dot(a_hi, wgb_ref[0]) + (_dot(a_hi, wgb_ref[1]) + _dot(a_lo, wgb_ref[0])) + bg_ref[...]
            la_buf[slot] = (jnp.minimum(z, 0.0) - jnp.log(1.0 + jnp.exp(-jnp.abs(z)))) / GLA_GATE_TEMP

        return [norm, proj_q, proj_k, proj_v, proj_g, gate]

    def chunk(c, slot):
        rows = pl.ds(c * c_len, c_len)
        b = _cumsum_via_mxu(tril, la_buf[slot, rows, :])
        b_last = b[c_len - 1:c_len, :]
        q = q_buf[slot, rows, :]
        k = k_buf[slot, rows, :]
        q_dec = (q * scale * jnp.exp(b)).astype(BF16)
        k_inv = (k * jnp.exp(-b)).astype(BF16)
        k_end = (k * jnp.exp(b_last - b)).astype(BF16)
        decay = jnp.exp(b_last)
        v = v_buf[slot, rows, :]
        outs = []
        for h in range(heads):
            ks = slice(h * dk, (h + 1) * dk)
            vh = v[:, h * dv:(h + 1) * dv]
            qd = q_dec[:, ks]
            attn = lax.dot_general(qd, k_inv[:, ks], NT_DIMS, preferred_element_type=F32)
            attn = jnp.where(causal, attn, 0.0).astype(BF16)
            st = st_ref[h]
            o = _dot(attn, vh) + lax.dot_general(qd, st.astype(BF16), NT_DIMS,
                                                 preferred_element_type=F32)
            st_ref[h] = st * decay[:, ks] + lax.dot_general(vh, k_end[:, ks], TN_DIMS,
                                                            preferred_element_type=F32)
            outs.append(o * lax.rsqrt(jnp.mean(o * o, axis=-1, keepdims=True) + EPS))
        o = jnp.concatenate(outs, axis=-1)
        o = o * gnw_ref[...] * _silu(g_buf[slot, rows, :])
        og_buf[slot, rows, :] = o.astype(BF16)

    def run_block(slot, out_rows, fill_next):
        n_chunks = tb // c_len
        for c in range(n_chunks):
            chunk(c, slot)
            if c < len(fill_next):
                fill_next[c]()
        for step in fill_next[n_chunks:]:
            step()
        o_ref[0, out_rows, :] = h_ref[0, out_rows, :] + _dot(og_buf[slot], wout_ref[...])

    @pl.when(pl.program_id(1) == 0)
    def _():
        st_ref[...] = jnp.zeros_like(st_ref)

    @pl.when((pl.program_id(0) == 0) & (pl.program_id(1) == 0))
    def _():
        for step in projection(lambda: h_ref[0, 0:tb, :], 0):
            step()

    run_block(0, pl.ds(0, tb), projection(lambda: h_ref[0, tb:2 * tb, :], 1))
    run_block(1, pl.ds(tb, tb), projection(lambda: hn_ref[0], 0))


def _lookahead_block(bsz, n_blk):
    def index_map(b, i):
        nxt = 2 * i + 2
        wrap = nxt >= n_blk
        return (jnp.minimum(b + wrap.astype(jnp.int32), bsz - 1), jnp.where(wrap, 0, nxt), 0)
    return index_map


def _gla_mixer(h, bsz, mix_norm, w_in, w_gate_b, b_gate, gn_w, w_out, tb=512):
    t, d = h.shape
    s = t // bsz
    dk_all, dv_all = GLA_HEADS * GLA_DK_H, GLA_HEADS * GLA_DV_H
    win = _pad_cols(w_in, 2 * dk_all + 2 * dv_all + LANES).astype(BF16)
    wgb = jnp.pad(w_gate_b, ((0, LANES - GLA_GATE_RANK), (0, 0)))
    wgb_hi = wgb.astype(BF16)
    wgb = jnp.stack([wgb_hi, (wgb - wgb_hi.astype(F32)).astype(BF16)])
    wout = w_out.astype(BF16)
    h3 = h.reshape(bsz, s, d)
    out = pl.pallas_call(
        functools.partial(_gla_kernel, tb=tb),
        grid=(bsz, s // (2 * tb)),
        in_specs=[pl.BlockSpec((1, 2 * tb, d), lambda b, i: (b, i, 0)),
                  pl.BlockSpec((1, tb, d), _lookahead_block(bsz, s // tb)),
                  _const_spec((1, d)), _const_spec(win.shape), _const_spec(wgb.shape),
                  _const_spec((1, dk_all)), _const_spec((1, dv_all)), _const_spec(wout.shape)],
        out_specs=pl.BlockSpec((1, 2 * tb, d), lambda b, i: (b, i, 0)),
        out_shape=jax.ShapeDtypeStruct((bsz, s, d), F32),
        scratch_shapes=[pltpu.VMEM((GLA_HEADS, GLA_DV_H, GLA_DK_H), F32),
                        pltpu.VMEM((tb, d), BF16),
                        pltpu.VMEM((2, tb, dv_all), BF16),
                        pltpu.VMEM((2, tb, dk_all), F32), pltpu.VMEM((2, tb, dk_all), F32),
                        pltpu.VMEM((2, tb, dv_all), BF16), pltpu.VMEM((2, tb, dv_all), F32),
                        pltpu.VMEM((2, tb, dk_all), F32)],
        compiler_params=_params("arbitrary", "arbitrary"),
        name="gla",
    )(h3, h3, mix_norm.reshape(1, d), win, wgb, b_gate.reshape(1, dk_all), gn_w.reshape(1, dv_all), wout)
    return out.reshape(t, d)


def _ssd_kernel(h_ref, hn_ref, mnw_ref, win_ref, cw_ref, cb_ref, dtb_ref, alog_ref, dsk_ref, nw_ref, e_ref,
                wout_ref, o_ref, st_ref, xn_ref, tail_ref, conv_ref, z_buf, xbc_buf, dt_buf, y_buf, *, tb):
    q_len, n_grp, hpg, p_dim, n_st = SSD_CHUNK, SSD_GROUPS, SSD_HPG, SSD_HEAD_DIM, SSD_STATE
    gw = hpg * p_dim
    z0, x0, d0 = 0, SSD_D_INNER, SSD_D_INNER + SSD_CONV_DIM
    conv_piece = SSD_CONV_DIM // 3

    def projection(load_rows, slot):
        def norm():
            xn_ref[...] = _rms(load_rows(), mnw_ref[...]).astype(BF16)

        def proj_z(lo, hi):
            z_buf[slot, :, lo:hi] = _dot(xn_ref[...], win_ref[:, z0 + lo:z0 + hi])

        def proj_x(lo, hi):
            xbc_buf[slot, :, lo:hi] = _dot(xn_ref[...], win_ref[:, x0 + lo:x0 + hi])

        def proj_dt():
            dt_buf[slot] = _dot(xn_ref[...], win_ref[:, d0:d0 + LANES])

        half = SSD_D_INNER // 2
        steps = [norm, functools.partial(proj_z, 0, half), functools.partial(proj_z, half, SSD_D_INNER)]
        steps += [functools.partial(proj_x, j * conv_piece, (j + 1) * conv_piece) for j in range(3)]
        return steps + [proj_dt]

    def conv_step(slot, lo, hi):
        u = jnp.concatenate([tail_ref[:, lo:hi], xbc_buf[slot, :, lo:hi]], axis=0)
        tiles = u.reshape((tb + SSD_HALO) // SSD_HALO, SSD_HALO, hi - lo)
        sub = lax.broadcasted_iota(jnp.int32, (tb // SSD_HALO, SSD_HALO, hi - lo), 1)
        acc = cb_ref[:, lo:hi] + cw_ref[SSD_CONV - 1:SSD_CONV, lo:hi] * tiles[1:]
        for delay in range(1, SSD_CONV):
            rot = pltpu.roll(tiles, delay, 1)
            acc = acc + cw_ref[SSD_CONV - 1 - delay:SSD_CONV - delay, lo:hi] * jnp.where(
                sub >= delay, rot[1:], rot[:-1])
        conv_ref[:, lo:hi] = _silu(acc).reshape(tb, hi - lo)
        tail_ref[:, lo:hi] = xbc_buf[slot, tb - SSD_HALO:tb, lo:hi]

    row = lax.broadcasted_iota(jnp.int32, (q_len, q_len), 0)
    col = lax.broadcasted_iota(jnp.int32, (q_len, q_len), 1)
    causal = col <= row
    tril = jnp.where(causal, 1.0, 0.0).astype(BF16)
    er = lax.broadcasted_iota(jnp.int32, (LANES, LANES), 0)
    ec = lax.broadcasted_iota(jnp.int32, (LANES, LANES), 1)
    eye = jnp.where(er == ec, 1.0, 0.0).astype(BF16)
    lane = lax.broadcasted_iota(jnp.int32, (q_len, LANES), 1)
    low_half = lane < p_dim
    a_neg = -jnp.exp(alog_ref[...])

    def chunk_steps(c, slot):
        rows = pl.ds(c * q_len, q_len)
        shared = {}

        def decays():
            dt = _softplus(dt_buf[slot, rows, :] + dtb_ref[...])
            da = dt * a_neg
            a_cs = _cumsum_via_mxu(tril, da)
            a_last = a_cs[q_len - 1:q_len, :]
            a_cs_t = _transpose_via_mxu(eye, a_cs)
            shared["shift_t"] = a_cs_t - jnp.log(_transpose_via_mxu(eye, dt))
            e_mat = e_ref[...]
            shared["a_cs"] = a_cs
            both = jnp.concatenate([jnp.exp(a_cs), jnp.exp(a_last - a_cs) * dt], axis=0).astype(BF16)
            both_x = _dot(both, e_mat)
            shared["ea_x"] = both_x[:q_len]
            shared["w_x"] = both_x[q_len:]
            shared["tot_x"] = _dot_exact_rhs(jnp.exp(a_last), e_mat, 2)

        def group(g):
            a_cs, shift_t, ea_x, w_x = shared["a_cs"], shared["shift_t"], shared["ea_x"], shared["w_x"]
            tot_x = shared["tot_x"]
            gs = slice(g * gw, (g + 1) * gw)
            x_g = conv_ref[rows, gs]
            b_g = conv_ref[rows, SSD_D_INNER + g * n_st:SSD_D_INNER + (g + 1) * n_st].astype(BF16)
            c_g = conv_ref[rows, SSD_D_INNER + (n_grp + g) * n_st:
                           SSD_D_INNER + (n_grp + g + 1) * n_st].astype(BF16)
            cb = lax.dot_general(c_g, b_g, NT_DIMS, preferred_element_type=F32)
            st = st_ref[g]
            y_off = _dot(c_g, st.astype(BF16)) * ea_x[:, gs]
            xw = (x_g * w_x[:, gs]).astype(BF16)
            st_ref[g] = st * tot_x[:, gs] + lax.dot_general(b_g, xw, TN_DIMS, preferred_element_type=F32)
            xb = x_g.astype(BF16)
            ys = []
            for pr in range(hpg // 2):
                h1 = g * hpg + 2 * pr
                x_pair = xb[:, pr * LANES:(pr + 1) * LANES]
                mats, xs = [], []
                for hh, keep in ((h1, low_half), (h1 + 1, jnp.logical_not(low_half))):
                    seg = a_cs[:, hh:hh + 1] - shift_t[hh:hh + 1, :]
                    mats.append((cb * jnp.where(causal, jnp.exp(seg), 0.0)).astype(BF16))
                    xs.append(jnp.where(keep, x_pair, jnp.zeros_like(x_pair)))
                ys.append(_dot(jnp.concatenate(mats, axis=1), jnp.concatenate(xs, axis=0)))
            y = jnp.concatenate(ys, axis=-1) + y_off + x_g * dsk_ref[:, gs]
            y = y * _silu(z_buf[slot, rows, gs])
            y = y * lax.rsqrt(jnp.mean(y * y, axis=-1, keepdims=True) + EPS) * nw_ref[:, gs]
            y_buf[slot, rows, gs] = y.astype(BF16)

        return [decays] + [functools.partial(group, g) for g in range(n_grp)]

    def run_block(slot, out_rows, fill_next):
        work = [functools.partial(conv_step, slot, j * conv_piece, (j + 1) * conv_piece) for j in range(3)]
        for c in range(tb // q_len):
            work += chunk_steps(c, slot)
        for j, step in enumerate(work):
            step()
            if j < len(fill_next):
                fill_next[j]()
        for step in fill_next[len(work):]:
            step()
        o_ref[0, out_rows, :] = h_ref[0, out_rows, :] + _dot(y_buf[slot], wout_ref[...])

    @pl.when(pl.program_id(1) == 0)
    def _():
        st_ref[...] = jnp.zeros_like(st_ref)
        tail_ref[...] = jnp.zeros_like(tail_ref)

    @pl.when((pl.program_id(0) == 0) & (pl.program_id(1) == 0))
    def _():
        for step in projection(lambda: h_ref[0, 0:tb, :], 0):
            step()

    run_block(0, pl.ds(0, tb), projection(lambda: h_ref[0, tb:2 * tb, :], 1))
    run_block(1, pl.ds(tb, tb), projection(lambda: hn_ref[0], 0))


def _ssd_mixer(h, bsz, mix_norm, w_in, conv_w, conv_b, dt_bias, a_log, d_skip, norm_w, w_out, tb=256):
    t, d = h.shape
    s = t // bsz
    win = _pad_cols(w_in, SSD_D_INNER + SSD_CONV_DIM + LANES).astype(BF16)
    wout = w_out.astype(BF16)
    head_of_lane = jnp.arange(SSD_D_INNER) // SSD_HEAD_DIM
    expand = (jnp.arange(LANES)[:, None] == head_of_lane[None, :]).astype(BF16)

    def pad_row(v):
        return jnp.pad(v, (0, LANES - v.shape[0])).reshape(1, LANES)

    h3 = h.reshape(bsz, s, d)
    out = pl.pallas_call(
        functools.partial(_ssd_kernel, tb=tb),
        grid=(bsz, s // (2 * tb)),
        in_specs=[pl.BlockSpec((1, 2 * tb, d), lambda b, i: (b, i, 0)),
                  pl.BlockSpec((1, tb, d), _lookahead_block(bsz, s // tb)),
                  _const_spec((1, d)), _const_spec(win.shape),
                  _const_spec((SSD_CONV, SSD_CONV_DIM)), _const_spec((1, SSD_CONV_DIM)),
                  _const_spec((1, LANES)), _const_spec((1, LANES)),
                  _const_spec((1, SSD_D_INNER)), _const_spec((1, SSD_D_INNER)),
                  _const_spec((LANES, SSD_D_INNER)), _const_spec(wout.shape)],
        out_specs=pl.BlockSpec((1, 2 * tb, d), lambda b, i: (b, i, 0)),
        out_shape=jax.ShapeDtypeStruct((bsz, s, d), F32),
        scratch_shapes=[pltpu.VMEM((SSD_GROUPS, SSD_STATE, SSD_HPG * SSD_HEAD_DIM), F32),
                        pltpu.VMEM((tb, d), BF16),
                        pltpu.VMEM((SSD_HALO, SSD_CONV_DIM), F32),
                        pltpu.VMEM((tb, SSD_CONV_DIM), F32),
                        pltpu.VMEM((2, tb, SSD_D_INNER), F32),
                        pltpu.VMEM((2, tb, SSD_CONV_DIM), F32),
                        pltpu.VMEM((2, tb, LANES), F32),
                        pltpu.VMEM((2, tb, SSD_D_INNER), BF16)],
        compiler_params=_params("arbitrary", "arbitrary"),
        name="ssd",
    )(h3, h3, mix_norm.reshape(1, d), win, conv_w, conv_b.reshape(1, -1), pad_row(dt_bias), pad_row(a_log),
      jnp.repeat(d_skip, SSD_HEAD_DIM).reshape(1, -1), norm_w.reshape(1, -1), expand, wout)
    return out.reshape(t, d)


def _mla_qkv_kernel(h_ref, pos_ref, nw_ref, win_ref, qn_ref, kvn_ref, wuq_ref, wuk_ref, wuv_ref, freq_ref,
                    q_ref, k_ref, v_ref):
    half = MLA_ROPE // 2
    o1, o2 = MLA_Q_LORA, MLA_Q_LORA + MLA_KV_LORA
    lat = _dot(_rms(h_ref[...], nw_ref[...]).astype(BF16), win_ref[...])
    cqn = _rms(lat[:, :o1], qn_ref[...]).astype(BF16)
    ckvn = _rms(lat[:, o1:o2], kvn_ref[...]).astype(BF16)
    tm = pos_ref.shape[0]
    lane = lax.broadcasted_iota(jnp.int32, (tm, LANES), 1)
    lane_h = lax.broadcasted_iota(jnp.int32, (tm // 2, LANES), 1)
    freq = freq_ref[...]
    ang2 = jnp.where(lane_h < MLA_ROPE, pos_ref[0:tm // 2, :] * freq, pos_ref[tm // 2:tm, :] * freq)
    cos2, sin2 = jnp.cos(ang2), jnp.sin(ang2)
    cos_a = jnp.concatenate([cos2, pltpu.roll(cos2, MLA_ROPE, 1)], axis=0)
    sin_a = jnp.concatenate([sin2, pltpu.roll(sin2, MLA_ROPE, 1)], axis=0)
    cos_v = jnp.where(lane < MLA_ROPE, cos_a, 0.0)
    sin_v = jnp.where(lane < half, -sin_a, jnp.where(lane < MLA_ROPE, sin_a, 0.0))

    def rope(t):
        return t * cos_v + pltpu.roll(t, MLA_ROPE, 1) * sin_v

    scale = MLA_QK ** -0.5 * math.log2(math.e)
    q = _dot(cqn, wuq_ref[...])
    k_rope = rope(lat[:, o2:o2 + LANES]).astype(BF16)
    k_nope = _dot(ckvn, wuk_ref[...])
    for h in range(MLA_HEADS):
        base = h * MLA_QK_PAD
        q_ref[:, base:base + MLA_NOPE] = (q[:, base:base + MLA_NOPE] * scale).astype(BF16)
        q_ref[:, base + MLA_NOPE:base + MLA_QK_PAD] = (
            rope(q[:, base + MLA_NOPE:base + MLA_QK_PAD]) * scale).astype(BF16)
        k_ref[:, base:base + MLA_NOPE] = k_nope[:, h * MLA_NOPE:(h + 1) * MLA_NOPE].astype(BF16)
        k_ref[:, base + MLA_NOPE:base + MLA_QK_PAD] = k_rope
    v = _dot(ckvn, wuv_ref[...])
    ones_col = jnp.where(lane == 0, 1.0, 0.0).astype(BF16)
    for h in range(MLA_HEADS):
        base = h * MLA_V_PAD
        v_ref[:, base:base + MLA_V] = v[:, h * MLA_V:(h + 1) * MLA_V].astype(BF16)
        v_ref[:, base + MLA_V:base + MLA_V_PAD] = ones_col


def _flash_kernel(q_ref, k_ref, v_ref, o_ref, acc_ref, *, tq, hps):
    qi = pl.program_id(2)
    acc_ref[...] = jnp.zeros_like(acc_ref)
    row = lax.broadcasted_iota(jnp.int32, (tq, tq), 0)
    col = lax.broadcasted_iota(jnp.int32, (tq, tq), 1)

    def step(blocks, ms, mask_last):
        key_rows = [pl.ds(pl.multiple_of(j * tq, tq), tq) for j in blocks]
        new_ms = []
        for hd in range(hps):
            q = q_ref[0, :, hd * MLA_QK_PAD:(hd + 1) * MLA_QK_PAD]
            scores = []
            for n, ks in enumerate(key_rows):
                k = k_ref[0, ks, hd * MLA_QK_PAD:(hd + 1) * MLA_QK_PAD]
                s = lax.dot_general(q, k, NT_DIMS, preferred_element_type=F32)
                if mask_last and n == len(key_rows) - 1:
                    s = jnp.where(col <= row, s, -jnp.inf)
                scores.append(s)
            m_new = ms[hd]
            for s in scores:
                m_new = jnp.maximum(m_new, jnp.max(s, axis=-1, keepdims=True))
            pv = None
            for s, ks in zip(scores, key_rows):
                p = jnp.exp2((s - m_new).astype(BF16))
                t = _dot(p, v_ref[0, ks, hd * MLA_V_PAD:(hd + 1) * MLA_V_PAD])
                pv = t if pv is None else pv + t
            acc_ref[hd] = jnp.exp2(ms[hd] - m_new) * acc_ref[hd] + pv
            new_ms.append(m_new)
        return tuple(new_ms)

    init = tuple(jnp.full((tq, 1), -jnp.inf, F32) for _ in range(hps))
    quads = qi // 4

    def two_pairs(j, cr):
        return step([4 * j + 2, 4 * j + 3], step([4 * j, 4 * j + 1], cr, False), False)

    ms = lax.fori_loop(0, quads, two_pairs, init)
    ms = lax.cond((qi // 2) % 2 == 1,
                  lambda cr: step([4 * quads, 4 * quads + 1], cr, False),
                  lambda cr: cr,
                  ms)
    lax.cond(qi % 2 == 1,
             lambda cr: step([qi - 1, qi], cr, True),
             lambda cr: step([qi], cr, True),
             ms)
    for hd in range(hps):
        acc = acc_ref[hd]
        o_ref[0, :, hd * MLA_V:(hd + 1) * MLA_V] = (acc[:, :MLA_V] / acc[:, MLA_V:MLA_V + 1]).astype(o_ref.dtype)


def _mla_mixer(h, bsz, positions, mix_norm, w_in, q_norm, w_uq, kv_norm, w_ukv, w_out, tm=512, tq=512):
    t, d = h.shape
    s = t // bsz
    half = MLA_ROPE // 2

    def swap_halves(w):
        return jnp.concatenate([w[..., half:], w[..., :half]], axis=-1)

    o1, o2 = MLA_Q_LORA, MLA_Q_LORA + MLA_KV_LORA
    win = jnp.concatenate([w_in, swap_halves(w_in[:, o2:])], axis=-1).astype(BF16)

    wq = w_uq.reshape(MLA_Q_LORA, MLA_HEADS, MLA_QK)
    wq_rope = wq[..., MLA_NOPE:]
    wuq = jnp.concatenate([wq[..., :MLA_NOPE], wq_rope, swap_halves(wq_rope)], axis=-1)
    wuq = wuq.reshape(MLA_Q_LORA, MLA_HEADS * MLA_QK_PAD).astype(BF16)
    wkv = w_ukv.reshape(MLA_KV_LORA, MLA_HEADS, MLA_NOPE + MLA_V)
    wuk = wkv[..., :MLA_NOPE].reshape(MLA_KV_LORA, MLA_HEADS * MLA_NOPE).astype(BF16)
    wuv = wkv[..., MLA_NOPE:].reshape(MLA_KV_LORA, MLA_HEADS * MLA_V).astype(BF16)
    inv_freq = 1.0 / (ROPE_THETA ** (jnp.arange(0, MLA_ROPE, 2, dtype=F32) / MLA_ROPE))
    freq = jnp.tile(inv_freq, LANES // half).reshape(1, LANES)
    pos = positions.astype(F32).reshape(t, 1)

    def rows(n):
        return pl.BlockSpec((tm, n), lambda i: (i, 0))

    qk_w = MLA_HEADS * MLA_QK_PAD
    v_w = MLA_HEADS * MLA_V
    vp_w = MLA_HEADS * MLA_V_PAD
    qf, kf, vf = pl.pallas_call(
        _mla_qkv_kernel,
        grid=(t // tm,),
        in_specs=[rows(d), rows(1), _const_spec((1, d)), _const_spec(win.shape),
                  _const_spec((1, MLA_Q_LORA)), _const_spec((1, MLA_KV_LORA)),
                  _const_spec(wuq.shape), _const_spec(wuk.shape), _const_spec(wuv.shape),
                  _const_spec((1, LANES))],
        out_specs=[rows(qk_w), rows(qk_w), rows(vp_w)],
        out_shape=[jax.ShapeDtypeStruct((t, qk_w), BF16), jax.ShapeDtypeStruct((t, qk_w), BF16),
                   jax.ShapeDtypeStruct((t, vp_w), BF16)],
        compiler_params=_params("parallel"),
        name="mla_qkv",
    )(h, pos, mix_norm.reshape(1, d), win, q_norm.reshape(1, -1), kv_norm.reshape(1, -1), wuq, wuk, wuv, freq)

    hps = FLASH_HEADS_PER_STEP
    o = pl.pallas_call(
        functools.partial(_flash_kernel, tq=tq, hps=hps),
        grid=(bsz, MLA_HEADS // hps, s // tq),
        in_specs=[pl.BlockSpec((1, tq, hps * MLA_QK_PAD), lambda b, hd, i: (b, i, hd)),
                  pl.BlockSpec((1, s, hps * MLA_QK_PAD), lambda b, hd, i: (b, 0, hd)),
                  pl.BlockSpec((1, s, hps * MLA_V_PAD), lambda b, hd, i: (b, 0, hd))],
        out_specs=pl.BlockSpec((1, tq, hps * MLA_V), lambda b, hd, i: (b, i, hd)),
        out_shape=jax.ShapeDtypeStruct((bsz, s, v_w), BF16),
        scratch_shapes=[pltpu.VMEM((hps, tq, MLA_V_PAD), F32)],
        compiler_params=_params("parallel", "parallel", "arbitrary"),
        name="mla_flash",
    )(qf.reshape(bsz, s, qk_w), kf.reshape(bsz, s, qk_w), vf.reshape(bsz, s, vp_w))
    return o.reshape(t, v_w), w_out.astype(BF16)


def kernel(x, positions, l0_ffn1_norm, l0_ffn1_w_gu, l0_ffn1_w_down, l0_mix_norm, l0_gla_w_in, l0_gla_w_gate_b, l0_gla_b_gate, l0_gla_norm, l0_gla_w_out, l0_ffn2_norm, l0_ffn2_w_gu, l0_ffn2_w_down, l1_ffn1_norm, l1_ffn1_w_gu, l1_ffn1_w_down, l1_mix_norm, l1_ssd_w_in, l1_ssd_conv_w, l1_ssd_conv_b, l1_ssd_dt_bias, l1_ssd_a_log, l1_ssd_d_skip, l1_ssd_norm, l1_ssd_w_out, l1_ffn2_norm, l1_ffn2_w_gu, l1_ffn2_w_down, l2_ffn1_norm, l2_ffn1_w_gu, l2_ffn1_w_down, l2_mix_norm, l2_mla_w_in, l2_mla_q_norm, l2_mla_w_uq, l2_mla_kv_norm, l2_mla_w_ukv, l2_mla_w_out, l2_ffn2_norm, l2_ffn2_w_gu, l2_ffn2_w_down, l3_ffn1_norm, l3_ffn1_w_gu, l3_ffn1_w_down, l3_mix_norm, l3_gla_w_in, l3_gla_w_gate_b, l3_gla_b_gate, l3_gla_norm, l3_gla_w_out, l3_ffn2_norm, l3_ffn2_w_gu, l3_ffn2_w_down, final_norm):
    bsz, s, d = x.shape
    h = x.reshape(bsz * s, d)

    h = _ffn(h, l0_ffn1_norm, l0_ffn1_w_gu, l0_ffn1_w_down)
    h = _gla_mixer(h, bsz, l0_mix_norm, l0_gla_w_in, l0_gla_w_gate_b, l0_gla_b_gate, l0_gla_norm, l0_gla_w_out)
    h = _ffn(h, l0_ffn2_norm, l0_ffn2_w_gu, l0_ffn2_w_down)

    h = _ffn(h, l1_ffn1_norm, l1_ffn1_w_gu, l1_ffn1_w_down)
    h = _ssd_mixer(h, bsz, l1_mix_norm, l1_ssd_w_in, l1_ssd_conv_w, l1_ssd_conv_b, l1_ssd_dt_bias,
                   l1_ssd_a_log, l1_ssd_d_skip, l1_ssd_norm, l1_ssd_w_out)
    h = _ffn(h, l1_ffn2_norm, l1_ffn2_w_gu, l1_ffn2_w_down)

    h = _ffn(h, l2_ffn1_norm, l2_ffn1_w_gu, l2_ffn1_w_down)
    attn = _mla_mixer(h, bsz, positions, l2_mix_norm, l2_mla_w_in, l2_mla_q_norm, l2_mla_w_uq,
                      l2_mla_kv_norm, l2_mla_w_ukv, l2_mla_w_out)
    h = _ffn(h, l2_ffn2_norm, l2_ffn2_w_gu, l2_ffn2_w_down, mixer_out=attn)

    h = _ffn(h, l3_ffn1_norm, l3_ffn1_w_gu, l3_ffn1_w_down)
    h = _gla_mixer(h, bsz, l3_mix_norm, l3_gla_w_in, l3_gla_w_gate_b, l3_gla_b_gate, l3_gla_norm, l3_gla_w_out)
    h = _ffn(h, l3_ffn2_norm, l3_ffn2_w_gu, l3_ffn2_w_down, final_w=final_norm)
    return h.reshape(bsz, s, d)
```
